```python
import math
import jax
import jax.numpy as jnp
from jax import lax
import numpy as np

D_MODEL = 2048
BATCH = 1
SEQ = 8192
DEPTH = 4

GRID_W = 64
CTX_LEN = 256
N_MIXERS = 4
NORM_EPS = 1e-6
ADA_CHUNKS = 6
ROPE_THETA = 10000.0
ROPE_DIM = 64
ATTN_BLOCK = 128

SSD_D_INNER = 2 * D_MODEL
SSD_HEAD_DIM = 64
SSD_N_HEADS = SSD_D_INNER // SSD_HEAD_DIM
SSD_GROUPS = 8
SSD_HEADS_PER_GROUP = SSD_N_HEADS // SSD_GROUPS
SSD_STATE = 128
SSD_CONV = 4
SSD_CHUNK = 128
SSD_CONV_DIM = SSD_D_INNER + 2 * SSD_GROUPS * SSD_STATE
SSD_IN_DIM = SSD_D_INNER + SSD_CONV_DIM + 2 * SSD_N_HEADS

LRU_WIDTH = D_MODEL
LRU_BLOCKS = 8
LRU_BLOCK = LRU_WIDTH // LRU_BLOCKS
LRU_CONV = 4
LRU_C = 8.0

MLA_HEADS = 16
MLA_Q_RANK = 768
MLA_KV_RANK = 512
MLA_NOPE = 128
MLA_ROPE = ROPE_DIM
MLA_V = 128
MLA_QK = MLA_NOPE + MLA_ROPE

SWA_Q_HEADS = 32
SWA_KV_HEADS = 8
SWA_GROUP = SWA_Q_HEADS // SWA_KV_HEADS
SWA_HEAD_DIM = ROPE_DIM
SWA_WINDOW = 128

FFN_HIDDEN = 5632
FFN_CONV = 3

kernel_name = "hybrid_interleaved_latent_diffusion_backbone"


def rmsnorm(x, g):
    xf = x.astype(jnp.float32)
    y = xf * lax.rsqrt(jnp.mean(xf * xf, axis=-1, keepdims=True) + NORM_EPS)
    return (y * g.astype(jnp.float32)).astype(x.dtype)


def modulate(h, shift, scale):
    return h * (1.0 + scale) + shift


def dwconv(x, w, b):
    k, ch = w.shape
    left = (k - 1) // 2
    y = lax.conv_general_dilated(x, w[:, None, :].astype(x.dtype), window_strides=(1,),
                                 padding=[(left, k - 1 - left)],
                                 dimension_numbers=('NWC', 'WIO', 'NWC'),
                                 feature_group_count=ch)
    return y + b


def axial_rope(n_tokens):
    rows = n_tokens // GRID_W
    row = jnp.repeat(jnp.arange(rows, dtype=jnp.float32), GRID_W)
    col = jnp.tile(jnp.arange(GRID_W, dtype=jnp.float32), rows)
    n_freq = ROPE_DIM // 4
    inv = ROPE_THETA ** (-jnp.arange(n_freq, dtype=jnp.float32) / n_freq)
    ang = jnp.concatenate([row[:, None] * inv, col[:, None] * inv], axis=-1)
    return jnp.cos(ang), jnp.sin(ang)


def apply_rope(x, cos, sin):
    half = x.shape[-1] // 2
    shape = (cos.shape[0],) + (1,) * (x.ndim - 3) + (half,)
    cos, sin = cos.reshape(shape), sin.reshape(shape)
    xf = x.astype(jnp.float32)
    x1, x2 = xf[..., :half], xf[..., half:]
    return jnp.concatenate([x1 * cos - x2 * sin, x2 * cos + x1 * sin], axis=-1).astype(x.dtype)


def softmax_attend(segments, scale, sink=None):
    logits = []
    for q, k, v, mask in segments:
        s = jnp.einsum('bqhgd,bkhd->bhgqk', q, k).astype(jnp.float32) * scale
        if mask is not None:
            s = jnp.where(mask, s, -jnp.inf)
        logits.append(s)
    s = jnp.concatenate(logits, axis=-1)
    m = jnp.max(s, axis=-1, keepdims=True)
    if sink is not None:
        sk = sink.astype(jnp.float32).reshape(1, s.shape[1], s.shape[2], 1, 1)
        m = jnp.maximum(m, sk)
        p = jnp.exp(s - m)
        denom = jnp.sum(p, axis=-1, keepdims=True) + jnp.exp(sk - m)
    else:
        p = jnp.exp(s - m)
        denom = jnp.sum(p, axis=-1, keepdims=True)
    v_all = jnp.concatenate([seg[2] for seg in segments], axis=1)
    w = (p / denom).astype(v_all.dtype)
    return jnp.einsum('bhgqk,bkhd->bqhgd', w, v_all)


def to_blocks(a):
    bsz, t = a.shape[:2]
    return jnp.moveaxis(a.reshape((bsz, t // ATTN_BLOCK, ATTN_BLOCK) + a.shape[2:]), 1, 0)


def from_blocks(o):
    nb, bsz, blk = o.shape[:3]
    return jnp.moveaxis(o, 0, 1).reshape(bsz, nb * blk, -1)


def linear_scan(a, b, h0):
    b = b.at[:, 0].add(a[:, 0] * h0)

    def combine(l, r):
        return l[0] * r[0], r[0] * l[1] + r[1]

    return lax.associative_scan(combine, (a, b), axis=1)[1]


def segsum(a):
    t = a.shape[-1]
    cs = jnp.cumsum(a, axis=-1)
    diff = cs[..., :, None] - cs[..., None, :]
    return jnp.where(jnp.tril(jnp.ones((t, t), dtype=bool)), diff, -jnp.inf)


def ssd_chunked(xs, adt, bm, cm, h0):
    bsz, t, g, e, p = xs.shape
    n = bm.shape[-1]
    nc = t // SSD_CHUNK
    xs = xs.reshape(bsz, nc, SSD_CHUNK, g, e, p)
    bm = bm.reshape(bsz, nc, SSD_CHUNK, g, n)
    cm = cm.reshape(bsz, nc, SSD_CHUNK, g, n)
    a = adt.astype(jnp.float32).reshape(bsz, nc, SSD_CHUNK, g, e).transpose(0, 3, 4, 1, 2)
    a_cs = jnp.cumsum(a, axis=-1)
    decay = jnp.exp(segsum(a))
    cb = jnp.einsum('bclgn,bcsgn->bcgls', cm, bm)
    y_diag = jnp.einsum('bcgls,bgecls,bcsgep->bclgep', cb, decay, xs)
    decay_states = jnp.exp(a_cs[..., -1:] - a_cs)
    states = jnp.einsum('bclgn,bgecl,bclgep->bcgepn', bm, decay_states, xs)
    states = jnp.concatenate([h0[:, None].astype(states.dtype), states], axis=1)
    chunk_decay = jnp.exp(segsum(jnp.pad(a_cs[..., -1], ((0, 0), (0, 0), (0, 0), (1, 0)))))
    new_states = jnp.einsum('bgezc,bcgepn->bzgepn', chunk_decay, states)
    states, final = new_states[:, :-1], new_states[:, -1]
    y_off = jnp.einsum('bclgn,bcgepn,bgecl->bclgep', cm, states, jnp.exp(a_cs))
    return (y_diag + y_off).reshape(bsz, t, g, e, p), final


def ssd_mixer(hc, hl, need_ctx, w_in, conv_w, conv_b, dt_bias, a_log, d_skip, norm_w, w_out):
    def project(h):
        bsz, t, _ = h.shape
        rest = h @ w_in[:, SSD_D_INNER:]
        xbc = jax.nn.silu(dwconv(rest[..., :SSD_CONV_DIM], conv_w, conv_b))
        gn = SSD_GROUPS * SSD_STATE
        xs = xbc[..., :SSD_D_INNER].reshape(bsz, t, SSD_GROUPS, SSD_HEADS_PER_GROUP, SSD_HEAD_DIM)
        bm = xbc[..., SSD_D_INNER:SSD_D_INNER + gn].reshape(bsz, t, SSD_GROUPS, SSD_STATE)
        cm = xbc[..., SSD_D_INNER + gn:].reshape(bsz, t, SSD_GROUPS, SSD_STATE)
        dt = rest[..., SSD_CONV_DIM:].reshape(bsz, t, 2, SSD_N_HEADS)
        return xs, bm, cm, dt

    def run(xs, bm, cm, dt, a, h0, reverse):
        if reverse:
            xs, bm, cm, dt = (jnp.flip(v, axis=1) for v in (xs, bm, cm, dt))
        y, h_t = ssd_chunked(xs * dt[..., None], dt * a, bm, cm, h0)
        if reverse:
            y = jnp.flip(y, axis=1)
        return y, h_t

    def gated_out(y, h):
        bsz, t, _ = h.shape
        z = h @ w_in[:, :SSD_D_INNER]
        g = (y.reshape(bsz, t, SSD_D_INNER) * jax.nn.silu(z)).astype(jnp.float32)
        g = g.reshape(bsz, t, SSD_GROUPS, -1)
        g = g * lax.rsqrt(jnp.mean(g * g, axis=-1, keepdims=True) + NORM_EPS)
        return (g.reshape(bsz, t, SSD_D_INNER) * norm_w).astype(h.dtype) @ w_out

    bsz, n_ctx = hc.shape[:2]
    n_lat = hl.shape[1]
    xs_c, b_c, c_c, dtr_c = project(hc)
    xs_l, b_l, c_l, dtr_l = project(hl)
    ys_c, ys_l = [], []
    for d in range(2):
        a = -jnp.exp(a_log[d].astype(jnp.float32)).reshape(SSD_GROUPS, SSD_HEADS_PER_GROUP)
        dt_c = jax.nn.softplus(dtr_c[:, :, d].astype(jnp.float32) + dt_bias[d]).reshape(
            bsz, n_ctx, SSD_GROUPS, SSD_HEADS_PER_GROUP)
        dt_l = jax.nn.softplus(dtr_l[:, :, d].astype(jnp.float32) + dt_bias[d]).reshape(
            bsz, n_lat, SSD_GROUPS, SSD_HEADS_PER_GROUP)
        h0 = jnp.zeros((bsz, SSD_GROUPS, SSD_HEADS_PER_GROUP, SSD_HEAD_DIM, SSD_STATE), jnp.float32)
        y_c, h_c = run(xs_c, b_c, c_c, dt_c, a, h0, d == 1)
        y_l, _ = run(xs_l, b_l, c_l, dt_l, a, h_c, d == 1)
        skip = d_skip[d].reshape(SSD_GROUPS, SSD_HEADS_PER_GROUP, 1)
        ys_c.append(y_c + skip * xs_c)
        ys_l.append(y_l + skip * xs_l)
    o_l = gated_out(ys_l[0] + ys_l[1], hl)
    o_c = gated_out(ys_c[0] + ys_c[1], hc) if need_ctx else None
    return o_c, o_l


def rglru_mixer(hc, hl, need_ctx, w_in, conv_w, conv_b, gate_w, gate_b, lam, w_out):
    def x_branch(h):
        return dwconv(h @ w_in[:, LRU_WIDTH:], conv_w, conv_b)

    def gate_branch(h):
        return jax.nn.gelu(h @ w_in[:, :LRU_WIDTH], approximate=True)

    def gates(u, d):
        bsz, t, _ = u.shape
        ub = u.reshape(bsz, t, LRU_BLOCKS, LRU_BLOCK)
        pre = jnp.einsum('btnk,znkj->zbtnj', ub, gate_w[d]) + gate_b[d][:, None, None]
        pre = pre.astype(jnp.float32).reshape(2, bsz, t, LRU_WIDTH)
        r, i = jax.nn.sigmoid(pre[0]), jax.nn.sigmoid(pre[1])
        log_a = -LRU_C * r * jax.nn.softplus(-lam[d].astype(jnp.float32))
        return jnp.exp(log_a), jnp.sqrt(-jnp.expm1(2.0 * log_a)) * (i * u.astype(jnp.float32))

    u_c, u_l = x_branch(hc), x_branch(hl)
    hs_c, hs_l = [], []
    for d in range(2):
        a_c, b_c = gates(u_c, d)
        a_l, b_l = gates(u_l, d)
        if d == 1:
            a_c, b_c, a_l, b_l = (jnp.flip(v, axis=1) for v in (a_c, b_c, a_l, b_l))
        h_c = linear_scan(a_c, b_c, jnp.zeros_like(b_c[:, 0]))
        h_l = linear_scan(a_l, b_l, h_c[:, -1])
        if d == 1:
            h_c, h_l = jnp.flip(h_c, axis=1), jnp.flip(h_l, axis=1)
        hs_c.append(h_c)
        hs_l.append(h_l)
    o_l = ((hs_l[0] + hs_l[1]) * gate_branch(hl)).astype(hl.dtype) @ w_out
    o_c = ((hs_c[0] + hs_c[1]) * gate_branch(hc)).astype(hc.dtype) @ w_out if need_ctx else None
    return o_c, o_l


def mla_mixer(hc, hl, need_ctx, cos, sin, w_in, q_norm, kv_norm, w_q_up, w_kv_up, w_out):
    def q_proj(h):
        bsz, t, _ = h.shape
        q = (rmsnorm(h @ w_in[:, :MLA_Q_RANK], q_norm) @ w_q_up).reshape(bsz, t, MLA_HEADS, MLA_QK)
        return q[..., :MLA_NOPE], q[..., MLA_NOPE:]

    def kv_proj(h):
        bsz, t, _ = h.shape
        a = h @ w_in[:, MLA_Q_RANK:]
        kv = (rmsnorm(a[..., :MLA_KV_RANK], kv_norm) @ w_kv_up).reshape(bsz, t, MLA_HEADS, MLA_NOPE + MLA_V)
        return kv[..., :MLA_NOPE], a[..., MLA_KV_RANK:], kv[..., MLA_NOPE:]

    def keys(k_nope, k_rope):
        return jnp.concatenate([k_nope, jnp.broadcast_to(k_rope, k_nope.shape[:3] + (MLA_ROPE,))], axis=-1)

    scale = MLA_QK ** -0.5
    kn_c, kr_c, v_c = kv_proj(hc)
    k_c = keys(kn_c, kr_c[:, :, None])
    kn_l, kr_l, v_l = kv_proj(hl)
    k_l = keys(kn_l, apply_rope(kr_l[:, :, None], cos, sin))
    qn_l, qr_l = q_proj(hl)
    q_plain = jnp.concatenate([qn_l, qr_l], axis=-1)[:, :, :, None]
    q_rot = jnp.concatenate([qn_l, apply_rope(qr_l, cos, sin)], axis=-1)[:, :, :, None]

    def block(args):
        qr, qp = args
        return softmax_attend([(qp, k_c, v_c, None), (qr, k_l, v_l, None)], scale)

    o_l = from_blocks(lax.map(block, (to_blocks(q_rot), to_blocks(q_plain)))) @ w_out
    o_c = None
    if need_ctx:
        qn_c, qr_c = q_proj(hc)
        q_c = jnp.concatenate([qn_c, qr_c], axis=-1)[:, :, :, None]
        o = softmax_attend([(q_c, k_c, v_c, None)], scale)
        o_c = o.reshape(hc.shape[0], hc.shape[1], -1) @ w_out
    return o_c, o_l


def swa_mixer(hc, hl, need_ctx, cos, sin, w_qkv, b_qkv, sink, w_out):
    qd = SWA_Q_HEADS * SWA_HEAD_DIM

    def q_proj(h):
        bsz, t, _ = h.shape
        return (h @ w_qkv[:, :qd] + b_qkv[:qd]).reshape(bsz, t, SWA_KV_HEADS, SWA_GROUP, SWA_HEAD_DIM)

    def kv_proj(h):
        bsz, t, _ = h.shape
        kv = (h @ w_qkv[:, qd:] + b_qkv[qd:]).reshape(bsz, t, 2, SWA_KV_HEADS, SWA_HEAD_DIM)
        return kv[:, :, 0], kv[:, :, 1]

    scale = SWA_HEAD_DIM ** -0.5
    n_lat = hl.shape[1]
    k_c, v_c = kv_proj(hc)
    q_l = q_proj(hl)
    k_l, v_l = kv_proj(hl)
    q_rot = apply_rope(q_l, cos, sin)
    pad = ((0, 0), (ATTN_BLOCK, ATTN_BLOCK), (0, 0), (0, 0))
    kp = jnp.pad(apply_rope(k_l, cos, sin), pad)
    vp = jnp.pad(v_l, pad)

    def block(args):
        b, qr, qp = args
        start = b * ATTN_BLOCK
        kb = lax.dynamic_slice_in_dim(kp, start, 3 * ATTN_BLOCK, axis=1)
        vb = lax.dynamic_slice_in_dim(vp, start, 3 * ATTN_BLOCK, axis=1)
        qpos = start + jnp.arange(ATTN_BLOCK)
        kpos = start - ATTN_BLOCK + jnp.arange(3 * ATTN_BLOCK)
        mask = ((jnp.abs(qpos[:, None] - kpos[None, :]) <= SWA_WINDOW)
                & (kpos >= 0)[None, :] & (kpos < n_lat)[None, :])
        return softmax_attend([(qp, k_c, v_c, None), (qr, kb, vb, mask)], scale, sink)

    nb = n_lat // ATTN_BLOCK
    o_l = from_blocks(lax.map(block, (jnp.arange(nb), to_blocks(q_rot), to_blocks(q_l)))) @ w_out
    o_c = None
    if need_ctx:
        o = softmax_attend([(q_proj(hc), k_c, v_c, None)], scale, sink)
        o_c = o.reshape(hc.shape[0], hc.shape[1], -1) @ w_out
    return o_c, o_l


def conv_ffn(h, w_up, conv_w, conv_b, w_down):
    u = dwconv(h @ w_up, conv_w, conv_b)
    g, v = jnp.split(u, 2, axis=-1)
    return (jax.nn.silu(g) * v) @ w_down


def setup_inputs(seed: int = 0) -> dict:
    key = jax.random.key(seed)
    keys = iter(jax.random.split(key, 64))
    f32 = jnp.float32
    D = D_MODEL

    def nrm(shape, scale):
        return jax.random.normal(next(keys), shape, f32) * scale

    def gain(shape):
        return 1.0 + nrm(shape, 0.05)

    n_ssd = len(range(0, DEPTH, N_MIXERS))
    n_lru = len(range(1, DEPTH, N_MIXERS))
    n_mla = len(range(2, DEPTH, N_MIXERS))
    n_swa = len(range(3, DEPTH, N_MIXERS))

    dt0 = jnp.exp(jax.random.uniform(next(keys), (n_ssd, 2, SSD_N_HEADS), f32, math.log(1e-3), math.log(1e-1)))
    dt_bias = dt0 + jnp.log(-jnp.expm1(-dt0))
    a_log = jnp.log(jax.random.uniform(next(keys), (n_ssd, 2, SSD_N_HEADS), f32, 1.0, 16.0))
    u = jax.random.uniform(next(keys), (n_lru, 2, LRU_WIDTH), f32, 0.9, 0.999)
    s = u ** (1.0 / LRU_C)
    lam = jnp.log(s) - jnp.log1p(-s)

    return {
        "x": nrm((BATCH, SEQ, D), 1.0),
        "c": nrm((BATCH, D), 1.0),
        "ctx": nrm((BATCH, CTX_LEN, D), 1.0),
        "c_ctx": nrm((D,), 1.0),
        "ada_w": nrm((DEPTH, D, ADA_CHUNKS * D), 0.5 * D ** -0.5),
        "ada_b": nrm((DEPTH, ADA_CHUNKS * D), 0.01),
        "norm_mix": gain((DEPTH, D)),
        "norm_ffn": gain((DEPTH, D)),
        "ffn_up": nrm((DEPTH, D, 2 * FFN_HIDDEN), D ** -0.5),
        "ffn_conv_w": nrm((DEPTH, FFN_CONV, 2 * FFN_HIDDEN), FFN_CONV ** -0.5),
        "ffn_conv_b": nrm((DEPTH, 2 * FFN_HIDDEN), 0.01),
        "ffn_down": nrm((DEPTH, FFN_HIDDEN, D), FFN_HIDDEN ** -0.5),
        "final_norm": gain((D,)),
        "ssd_in": nrm((n_ssd, D, SSD_IN_DIM), D ** -0.5),
        "ssd_conv_w": nrm((n_ssd, SSD_CONV, SSD_CONV_DIM), SSD_CONV ** -0.5),
        "ssd_conv_b": nrm((n_ssd, SSD_CONV_DIM), 0.01),
        "ssd_dt_bias": dt_bias,
        "ssd_a_log": a_log,
        "ssd_d": gain((n_ssd, 2, SSD_N_HEADS)),
        "ssd_norm": gain((n_ssd, SSD_D_INNER)),
        "ssd_out": nrm((n_ssd, SSD_D_INNER, D), SSD_D_INNER ** -0.5),
        "lru_in": nrm((n_lru, D, 2 * LRU_WIDTH), D ** -0.5),
        "lru_conv_w": nrm((n_lru, LRU_CONV, LRU_WIDTH), LRU_CONV ** -0.5),
        "lru_conv_b": nrm((n_lru, LRU_WIDTH), 0.01),
        "lru_gate_w": nrm((n_lru, 2, 2, LRU_BLOCKS, LRU_BLOCK, LRU_BLOCK), LRU_BLOCK ** -0.5),
        "lru_gate_b": nrm((n_lru, 2, 2, LRU_BLOCKS, LRU_BLOCK), 0.01),
        "lru_lambda": lam,
        "lru_out": nrm((n_lru, LRU_WIDTH, D), LRU_WIDTH ** -0.5),
        "mla_in": nrm((n_mla, D, MLA_Q_RANK + MLA_KV_RANK + MLA_ROPE), D ** -0.5),
        "mla_q_norm": gain((n_mla, MLA_Q_RANK)),
        "mla_kv_norm": gain((n_mla, MLA_KV_RANK)),
        "mla_q_up": nrm((n_mla, MLA_Q_RANK, MLA_HEADS * MLA_QK), MLA_Q_RANK ** -0.5),
        "mla_kv_up": nrm((n_mla, MLA_KV_RANK, MLA_HEADS * (MLA_NOPE + MLA_V)), MLA_KV_RANK ** -0.5),
        "mla_out": nrm((n_mla, MLA_HEADS * MLA_V, D), (MLA_HEADS * MLA_V) ** -0.5),
        "swa_qkv": nrm((n_swa, D, (SWA_Q_HEADS + 2 * SWA_KV_HEADS) * SWA_HEAD_DIM), D ** -0.5),
        "swa_qkv_b": nrm((n_swa, (SWA_Q_HEADS + 2 * SWA_KV_HEADS) * SWA_HEAD_DIM), 0.01),
        "swa_sink": nrm((n_swa, SWA_Q_HEADS), 1.0),
        "swa_out": nrm((n_swa, SWA_Q_HEADS * SWA_HEAD_DIM, D), (SWA_Q_HEADS * SWA_HEAD_DIM) ** -0.5),
    }


def reference(x, c, ctx, c_ctx, ada_w, ada_b, norm_mix, norm_ffn, ffn_up, ffn_conv_w, ffn_conv_b,
              ffn_down, final_norm, ssd_in, ssd_conv_w, ssd_conv_b, ssd_dt_bias, ssd_a_log, ssd_d,
              ssd_norm, ssd_out, lru_in, lru_conv_w, lru_conv_b, lru_gate_w, lru_gate_b, lru_lambda,
              lru_out, mla_in, mla_q_norm, mla_kv_norm, mla_q_up, mla_kv_up, mla_out, swa_qkv,
              swa_qkv_b, swa_sink, swa_out):
    bsz, n_lat, _ = x.shape
    cos, sin = axial_rope(n_lat)
    sc = jax.nn.silu(c)
    scc = jax.nn.silu(c_ctx)
    xl, xc = x, ctx
    for i in range(DEPTH):
        kind, j = i % N_MIXERS, i // N_MIXERS
        need_ctx = i < DEPTH - 1
        mod_l = (sc @ ada_w[i] + ada_b[i]).reshape(bsz, ADA_CHUNKS, 1, D_MODEL)
        mod_c = (scc @ ada_w[i] + ada_b[i]).reshape(ADA_CHUNKS, 1, 1, D_MODEL)
        hl = modulate(rmsnorm(xl, norm_mix[i]), mod_l[:, 0], mod_l[:, 1])
        hc = modulate(rmsnorm(xc, norm_mix[i]), mod_c[0], mod_c[1])
        if kind == 0:
            oc, ol = ssd_mixer(hc, hl, need_ctx, ssd_in[j], ssd_conv_w[j], ssd_conv_b[j], ssd_dt_bias[j],
                               ssd_a_log[j], ssd_d[j], ssd_norm[j], ssd_out[j])
        elif kind == 1:
            oc, ol = rglru_mixer(hc, hl, need_ctx, lru_in[j], lru_conv_w[j], lru_conv_b[j], lru_gate_w[j],
                                 lru_gate_b[j], lru_lambda[j], lru_out[j])
        elif kind == 2:
            oc, ol = mla_mixer(hc, hl, need_ctx, cos, sin, mla_in[j], mla_q_norm[j], mla_kv_norm[j],
                               mla_q_up[j], mla_kv_up[j], mla_out[j])
        else:
            oc, ol = swa_mixer(hc, hl, need_ctx, cos, sin, swa_qkv[j], swa_qkv_b[j], swa_sink[j], swa_out[j])
        xl = (xl + mod_l[:, 2] * ol).astype(x.dtype)
        hl = modulate(rmsnorm(xl, norm_ffn[i]), mod_l[:, 3], mod_l[:, 4])
        xl = (xl + mod_l[:, 5] * conv_ffn(hl, ffn_up[i], ffn_conv_w[i], ffn_conv_b[i], ffn_down[i])).astype(x.dtype)
        if need_ctx:
            xc = (xc + mod_c[2] * oc).astype(ctx.dtype)
            hc = modulate(rmsnorm(xc, norm_ffn[i]), mod_c[3], mod_c[4])
            xc = (xc + mod_c[5] * conv_ffn(hc, ffn_up[i], ffn_conv_w[i], ffn_conv_b[i], ffn_down[i])).astype(ctx.dtype)
    return rmsnorm(xl, final_norm)
```

```python
import functools
import math
from typing import NamedTuple

import jax
import jax.numpy as jnp
from jax import lax
from jax.experimental import pallas as pl
from jax.experimental.pallas import tpu as pltpu

F32 = jnp.float32
BF16 = jnp.bfloat16

GRID_W = 64
NORM_EPS = 1e-6
ADA_CHUNKS = 6
ROPE_THETA = 10000.0
ROPE_DIM = 64
SSD_HEAD_DIM = 64
SSD_GROUPS = 8
SSD_HEADS_PER_GROUP = 8
SSD_STATE = 128
SSD_CHUNK = 128
LRU_BLOCKS = 8
LRU_C = 8.0
MLA_HEADS = 16
MLA_Q_RANK = 768
MLA_KV_RANK = 512
MLA_NOPE = 128
MLA_V = 128
SWA_Q_HEADS = 32
SWA_KV_HEADS = 8
SWA_GROUP = 4
SWA_HEAD_DIM = 64
SWA_WINDOW = 128
ATTN_BLOCK = 128

LANES = 128
BF16_ROWS = 16
TM = 512
TE = 256
VMEM_LIMIT = 56 * 2 ** 20


class Layout(NamedTuple):
    n_lat: int
    n_ctx: int
    tp: int

    @property
    def n_valid(self):
        return self.n_lat + self.n_ctx


def make_layout(n_lat, n_ctx):
    assert n_lat % TM == 0 and n_ctx % SSD_CHUNK == 0
    ctx_pad = -(-n_ctx // TM) * TM
    return Layout(n_lat, n_ctx, n_lat + ctx_pad)


def _cp(sem):
    return pltpu.CompilerParams(dimension_semantics=sem, vmem_limit_bytes=VMEM_LIMIT)


def _silu(x):
    return x * jax.nn.sigmoid(x)


def _softplus(x):
    return jnp.maximum(x, 0.0) + jnp.log1p(jnp.exp(-jnp.abs(x)))


def _dot(a, b):
    return jnp.dot(a, b, preferred_element_type=F32)


def _dot_nt(a, b):
    return lax.dot_general(a, b, (((1,), (1,)), ((), ())), preferred_element_type=F32)


def _split3(x):
    hi = x.astype(BF16)
    r = x - hi.astype(F32)
    mid = r.astype(BF16)
    lo = (r - mid.astype(F32)).astype(BF16)
    return hi, mid, lo


def _rope128(v, c, s1, s2):
    return v * c + pltpu.roll(v, 96, 1) * s1 + pltpu.roll(v, 32, 1) * s2


def _adaln_kernel(cv_ref, w_ref, b_ref, o_ref):
    s = _silu(cv_ref[...]).astype(BF16)
    o_ref[...] = _dot(s, w_ref[...].astype(BF16)) + b_ref[...]


def adaln(cv, ada_w, ada_b):
    depth, d, n = ada_w.shape
    tn = 1024
    return pl.pallas_call(
        _adaln_kernel,
        grid=(depth, n // tn),
        in_specs=[pl.BlockSpec((8, d), lambda l, j: (0, 0)),
                  pl.BlockSpec((None, d, tn), lambda l, j: (l, 0, j)),
                  pl.BlockSpec((None, 1, tn), lambda l, j: (l, 0, j))],
        out_specs=pl.BlockSpec((None, 8, tn), lambda l, j: (l, 0, j)),
        out_shape=jax.ShapeDtypeStruct((depth, 8, n), F32),
        compiler_params=_cp(("parallel", "parallel")),
        name="adaln",
    )(cv, ada_w, ada_b.reshape(depth, 1, n))


def _norm_mod_kernel(x_ref, g_ref, mod_ref, h_ref, *, shift_idx, scale_idx, n_valid, tm):
    x = x_ref[...]
    y = x * lax.rsqrt(jnp.mean(x * x, axis=-1, keepdims=True) + NORM_EPS) * g_ref[...]
    h = y * (1.0 + mod_ref[scale_idx:scale_idx + 1, :]) + mod_ref[shift_idx:shift_idx + 1, :]
    row = pl.program_id(0) * tm + lax.broadcasted_iota(jnp.int32, (tm, 1), 0)
    h_ref[...] = jnp.where(row < n_valid, h, 0.0).astype(BF16)


def norm_mod(x, g, mod, shift_idx, scale_idx, lay):
    tp, d = x.shape
    tm = TE
    return pl.pallas_call(
        functools.partial(_norm_mod_kernel, shift_idx=shift_idx, scale_idx=scale_idx,
                          n_valid=lay.n_valid, tm=tm),
        grid=(tp // tm,),
        in_specs=[pl.BlockSpec((tm, d), lambda i: (i, 0)),
                  pl.BlockSpec((1, d), lambda i: (0, 0)),
                  pl.BlockSpec((None, 8, d), lambda i: ((i * tm >= lay.n_lat).astype(jnp.int32), 0, 0))],
        out_specs=pl.BlockSpec((tm, d), lambda i: (i, 0)),
        out_shape=jax.ShapeDtypeStruct((tp, d), BF16),
        compiler_params=_cp(("parallel",)),
        name="norm_mod",
    )(x, g.reshape(1, d), mod)


def _rmsnorm_kernel(x_ref, g_ref, o_ref):
    x = x_ref[...]
    o_ref[...] = x * lax.rsqrt(jnp.mean(x * x, axis=-1, keepdims=True) + NORM_EPS) * g_ref[...]


def rmsnorm_rows(x, g, rows):
    d = x.shape[1]
    tm = TE
    return pl.pallas_call(
        _rmsnorm_kernel,
        grid=(rows // tm,),
        in_specs=[pl.BlockSpec((tm, d), lambda i: (i, 0)), pl.BlockSpec((1, d), lambda i: (0, 0))],
        out_specs=pl.BlockSpec((tm, d), lambda i: (i, 0)),
        out_shape=jax.ShapeDtypeStruct((rows, d), F32),
        compiler_params=_cp(("parallel",)),
        name="final_norm",
    )(x, g.reshape(1, d))


def _mm_kernel(a_ref, w_ref, o_ref):
    o_ref[...] = _dot(a_ref[...], w_ref[...]).astype(o_ref.dtype)


def mm(a, w, out_dtype=F32, tn=None, name="mm"):
    m, k = a.shape
    n = w.shape[1]
    tn = n if tn is None else tn
    return pl.pallas_call(
        _mm_kernel,
        grid=(m // TM, n // tn),
        in_specs=[pl.BlockSpec((TM, k), lambda i, j: (i, 0)),
                  pl.BlockSpec((k, tn), lambda i, j: (0, j))],
        out_specs=pl.BlockSpec((TM, tn), lambda i, j: (i, j)),
        out_shape=jax.ShapeDtypeStruct((m, n), out_dtype),
        compiler_params=_cp(("parallel", "parallel")),
        name=name,
    )(a, w)


def _mm_resid_kernel(a_ref, w_ref, x_ref, mod_ref, o_ref, *, gate_idx):
    acc = _dot(a_ref[...], w_ref[...])
    o_ref[...] = x_ref[...] + mod_ref[gate_idx:gate_idx + 1, :] * acc


def mm_resid(a, w, x, mod, gate_idx, lay, name="mm_resid"):
    m, k = a.shape
    n = w.shape[1]
    tn = 512
    return pl.pallas_call(
        functools.partial(_mm_resid_kernel, gate_idx=gate_idx),
        grid=(m // TM, n // tn),
        in_specs=[pl.BlockSpec((TM, k), lambda i, j: (i, 0)),
                  pl.BlockSpec((k, tn), lambda i, j: (0, j)),
                  pl.BlockSpec((TM, tn), lambda i, j: (i, j)),
                  pl.BlockSpec((None, 8, tn), lambda i, j: ((i * TM >= lay.n_lat).astype(jnp.int32), 0, j))],
        out_specs=pl.BlockSpec((TM, tn), lambda i, j: (i, j)),
        out_shape=jax.ShapeDtypeStruct((m, n), F32),
        compiler_params=_cp(("parallel", "parallel")),
        name=name,
    )(a, w, x, mod)


HALO = BF16_ROWS


def _mm_conv_kernel(*refs, ksize, gated, act, tm, n_lat_tiles, n_tiles):
    nw = 2 if gated else 1
    a_ref, ap_ref, an_ref = refs[:3]
    w_refs = refs[3:3 + nw]
    cw_refs = refs[3 + nw:3 + 2 * nw]
    cb_refs = refs[3 + 2 * nw:3 + 3 * nw]
    o_ref = refs[3 + 3 * nw]
    hbuf = refs[4 + 3 * nw]
    u_scrs = refs[5 + 3 * nw:]
    i = pl.program_id(0)

    @pl.when(pl.program_id(1) == 0)
    def _():
        first = (i == 0) | (i == n_lat_tiles)
        last = (i == n_lat_tiles - 1) | (i == n_tiles - 1)
        prev = ap_ref[...]
        nxt = an_ref[...]
        hbuf[0:HALO, :] = jnp.where(first, jnp.zeros_like(prev), prev)
        hbuf[HALO:HALO + tm, :] = a_ref[...]
        hbuf[HALO + tm:, :] = jnp.where(last, jnp.zeros_like(nxt), nxt)

    left = (ksize - 1) // 2

    def conv(w_ref, cw_ref, cb_ref, u_scr):
        u_scr[...] = _dot(hbuf[...], w_ref[...])
        acc = cb_ref[...] + cw_ref[0:1, :] * u_scr[pl.ds(HALO - left, tm), :]
        for kk in range(1, ksize):
            acc = acc + cw_ref[kk:kk + 1, :] * u_scr[pl.ds(HALO - left + kk, tm), :]
        return acc

    if gated:
        g = conv(w_refs[0], cw_refs[0], cb_refs[0], u_scrs[0])
        v = conv(w_refs[1], cw_refs[1], cb_refs[1], u_scrs[1])
        out = _silu(g) * v
    else:
        out = conv(w_refs[0], cw_refs[0], cb_refs[0], u_scrs[0])
        if act == "silu":
            out = _silu(out)
    o_ref[...] = out.astype(o_ref.dtype)


def mm_conv(a, w, conv_w, conv_b, lay, *, tn, out_dtype, gated=False, act=None,
            group_major=False, col0=0, ncols=None, name="mm_conv"):
    tp, k = a.shape
    ksize = conv_w.shape[0]
    ncols = ncols if ncols is not None else w.shape[1]
    nj = ncols // tn
    c0 = col0 // tn
    assert col0 % tn == 0 and ncols % tn == 0
    n_tiles = tp // TM
    hb = TM // HALO
    a_specs = [pl.BlockSpec((TM, k), lambda i, j: (i, 0)),
               pl.BlockSpec((HALO, k), lambda i, j: (jnp.maximum(i * hb - 1, 0), 0)),
               pl.BlockSpec((HALO, k), lambda i, j: (jnp.minimum((i + 1) * hb, tp // HALO - 1), 0))]
    offs = [c0, c0 + nj] if gated else [c0]
    w_specs = [pl.BlockSpec((k, tn), lambda i, j, o=o: (0, o + j)) for o in offs]
    cw_specs = [pl.BlockSpec((ksize, tn), lambda i, j, o=o: (0, o + j)) for o in offs]
    cb_specs = [pl.BlockSpec((1, tn), lambda i, j, o=o: (0, o + j)) for o in offs]
    nw = len(offs)
    if group_major:
        out_spec = pl.BlockSpec((None, TM, tn), lambda i, j: (j, i, 0))
        out_shape = jax.ShapeDtypeStruct((nj, tp, tn), out_dtype)
    else:
        out_spec = pl.BlockSpec((TM, tn), lambda i, j: (i, j))
        out_shape = jax.ShapeDtypeStruct((tp, ncols), out_dtype)
    cb2 = conv_b.reshape(1, -1)
    return pl.pallas_call(
        functools.partial(_mm_conv_kernel, ksize=ksize, gated=gated, act=act, tm=TM,
                          n_lat_tiles=lay.n_lat // TM, n_tiles=n_tiles),
        grid=(n_tiles, nj),
        in_specs=a_specs + w_specs + cw_specs + cb_specs,
        out_specs=out_spec,
        out_shape=out_shape,
        scratch_shapes=[pltpu.VMEM((TM + 2 * HALO, k), BF16)]
        + [pltpu.VMEM((TM + 2 * HALO, tn), F32) for _ in range(nw)],
        compiler_params=_cp(("parallel", "arbitrary")),
        name=name,
    )(a, a, a, *([w] * nw), *([conv_w] * nw), *([cb2] * nw))


def _chunk_index(k, lay, reverse, chunk):
    n_lc = lay.n_lat // chunk
    n_cc = lay.n_ctx // chunk
    if reverse:
        valid = n_lc + n_cc - 1 - k
    else:
        valid = jnp.where(k < n_cc, n_lc + k, k - n_cc)
    return jnp.where(k < n_lc + n_cc, valid, k)


def _ssd_scan_kernel(xs_ref, b_ref, c_ref, bt_ref, dtc_ref, dtr_ref, biasc_ref, biasr_ref,
                     ac_ref, ar_ref, dsk_ref, y_ref, h_scr, *, reverse, n_valid_chunks):
    k = pl.program_id(1)
    L = SSD_CHUNK

    @pl.when(k == 0)
    def _():
        h_scr[...] = jnp.zeros_like(h_scr)

    @pl.when(k >= n_valid_chunks)
    def _():
        y_ref[...] = jnp.zeros_like(y_ref)

    @pl.when(k < n_valid_chunks)
    def _():
        li = lax.broadcasted_iota(jnp.int32, (L, L), 0)
        si = lax.broadcasted_iota(jnp.int32, (L, L), 1)
        if reverse:
            mask = si >= li
            mask_t = li >= si
        else:
            mask = si <= li
            mask_t = li <= si
        tri = jnp.where(mask, 1.0, 0.0).astype(BF16)
        tri_t = jnp.where(mask_t, 1.0, 0.0).astype(BF16)

        dtc = _softplus(dtc_ref[...] + biasc_ref[...])
        adt_c = dtc * ac_ref[...]
        hi, mid, lo = _split3(adt_c)
        acs_c = _dot(tri, hi) + _dot(tri, mid) + _dot(tri, lo)
        dtr = _softplus(dtr_ref[...] + biasr_ref[...])
        adt_r = dtr * ar_ref[...]
        hi, mid, lo = _split3(adt_r)
        acs_r = _dot(hi, tri_t) + _dot(mid, tri_t) + _dot(lo, tri_t)
        tot_r = acs_r[:, 0:1] if reverse else acs_r[:, L - 1:L]
        wdt_r = jnp.exp(tot_r - acs_r) * dtr
        etot = jnp.exp(tot_r)
        ecol = jnp.exp(acs_c)

        cb16 = c_ref[...]
        bt16 = bt_ref[...]
        cbm = jnp.where(mask, _dot(cb16, bt16), 0.0)
        cf = cb16.astype(F32)
        btf = bt16.astype(F32)
        lo_half = lax.broadcasted_iota(jnp.int32, (L, LANES), 1) < SSD_HEAD_DIM

        for p in range(SSD_HEADS_PER_GROUP // 2):
            cols = slice(p * LANES, (p + 1) * LANES)
            xp = xs_ref[:, cols]
            xpb = xp.astype(BF16)
            hp = h_scr[p]
            hpb = hp.astype(BF16)
            ys, hs = [], []
            for e2 in range(2):
                e = 2 * p + e2
                col = acs_c[:, e:e + 1]
                row = acs_r[e:e + 1, :]
                ld = (cbm * jnp.exp(jnp.minimum(col - row, 0.0)) * dtr[e:e + 1, :]).astype(BF16)
                lo_m = (cf * ecol[:, e:e + 1]).astype(BF16)
                ls = (btf * wdt_r[e:e + 1, :]).astype(BF16)
                ys.append(_dot(ld, xpb) + _dot(lo_m, hpb))
                hs.append(_dot(ls, xpb) + etot[e:e + 1, :] * hp)
            y_ref[:, cols] = jnp.where(lo_half, ys[0], ys[1]) + dsk_ref[:, cols] * xp
            h_scr[p] = jnp.where(lo_half, hs[0], hs[1])


def ssd_scan(xs, bc, bt, dtc, dtr, bias, a_neg, dskip, lay, reverse):
    g, tp, gw = xs.shape
    L = SSD_CHUNK
    n_chunks = tp // L
    n_valid = lay.n_valid // L
    cidx = functools.partial(_chunk_index, lay=lay, reverse=reverse, chunk=L)
    hpg = SSD_HEADS_PER_GROUP

    def lane_pad(v):
        return jnp.pad(v.reshape(g, 1, hpg), ((0, 0), (0, 0), (0, LANES - hpg)))

    biasc, ac = lane_pad(bias), lane_pad(a_neg)
    biasr, ar = bias.reshape(g, hpg, 1), a_neg.reshape(g, hpg, 1)
    dsk = jnp.repeat(dskip, SSD_HEAD_DIM).reshape(g, 1, gw)
    return pl.pallas_call(
        functools.partial(_ssd_scan_kernel, reverse=reverse, n_valid_chunks=n_valid),
        grid=(g, n_chunks),
        in_specs=[pl.BlockSpec((None, L, gw), lambda gi, k: (gi, cidx(k), 0)),
                  pl.BlockSpec((None, L, SSD_STATE), lambda gi, k: (gi, cidx(k), 0)),
                  pl.BlockSpec((None, L, SSD_STATE), lambda gi, k: (g + gi, cidx(k), 0)),
                  pl.BlockSpec((None, SSD_STATE, L), lambda gi, k: (gi, 0, cidx(k))),
                  pl.BlockSpec((None, L, LANES), lambda gi, k: (gi, cidx(k), 0)),
                  pl.BlockSpec((None, hpg, L), lambda gi, k: (gi, 0, cidx(k))),
                  pl.BlockSpec((None, 1, LANES), lambda gi, k: (gi, 0, 0)),
                  pl.BlockSpec((None, hpg, 1), lambda gi, k: (gi, 0, 0)),
                  pl.BlockSpec((None, 1, LANES), lambda gi, k: (gi, 0, 0)),
                  pl.BlockSpec((None, hpg, 1), lambda gi, k: (gi, 0, 0)),
                  pl.BlockSpec((None, 1, gw), lambda gi, k: (gi, 0, 0))],
        out_specs=pl.BlockSpec((None, L, gw), lambda gi, k: (gi, cidx(k), 0)),
        out_shape=jax.ShapeDtypeStruct((g, tp, gw), F32),
        scratch_shapes=[pltpu.VMEM((hpg // 2, SSD_STATE, LANES), F32)],
        compiler_params=_cp(("parallel", "arbitrary")),
        name="ssd_scan_rev" if reverse else "ssd_scan_fwd",
    )(xs, bc, bc, bt, dtc, dtr, biasc, biasr, ac, ar, dsk)


def _ssd_gate_kernel(yf_ref, yb_ref, z_ref, nw_ref, a_ref):
    gw = yf_ref.shape[-1]
    for g in range(SSD_GROUPS):
        cols = slice(g * gw, (g + 1) * gw)
        gz = (yf_ref[g] + yb_ref[g]) * _silu(z_ref[:, cols])
        gn = gz * lax.rsqrt(jnp.mean(gz * gz, axis=-1, keepdims=True) + NORM_EPS)
        a_ref[:, cols] = (gn * nw_ref[:, cols]).astype(BF16)


def ssd_gate(yf, yb, z, norm_w):
    g, tp, gw = yf.shape
    d_inner = g * gw
    tm = TE
    return pl.pallas_call(
        _ssd_gate_kernel,
        grid=(tp // tm,),
        in_specs=[pl.BlockSpec((g, tm, gw), lambda i: (0, i, 0)),
                  pl.BlockSpec((g, tm, gw), lambda i: (0, i, 0)),
                  pl.BlockSpec((tm, d_inner), lambda i: (i, 0)),
                  pl.BlockSpec((1, d_inner), lambda i: (0, 0))],
        out_specs=pl.BlockSpec((tm, d_inner), lambda i: (i, 0)),
        out_shape=jax.ShapeDtypeStruct((tp, d_inner), BF16),
        compiler_params=_cp(("parallel",)),
        name="ssd_gate",
    )(yf, yb, z, norm_w.reshape(1, d_inner))


def ssd_mixer(h, lay, w_in, conv_w, conv_b, dt_bias, a_log, d_skip, norm_w):
    d_inner = SSD_GROUPS * SSD_HEADS_PER_GROUP * SSD_HEAD_DIM
    gn = SSD_GROUPS * SSD_STATE
    conv_dim = d_inner + 2 * gn
    n_heads = SSD_GROUPS * SSD_HEADS_PER_GROUP
    tp = h.shape[0]
    w16 = w_in.astype(BF16)
    w_rest = w16[:, d_inner:d_inner + conv_dim]
    z = mm(h, w16[:, :d_inner], tn=512, name="ssd_z")
    xs = mm_conv(h, w_rest, conv_w, conv_b, lay, tn=SSD_HEADS_PER_GROUP * SSD_HEAD_DIM, out_dtype=F32,
                 act="silu", group_major=True, col0=0, ncols=d_inner, name="ssd_x")
    bc = mm_conv(h, w_rest, conv_w, conv_b, lay, tn=SSD_STATE, out_dtype=BF16, act="silu",
                 group_major=True, col0=d_inner, ncols=2 * gn, name="ssd_bc")
    dt_pre = mm(h, w16[:, d_inner + conv_dim:], name="ssd_dt")
    bt = jnp.transpose(bc[:SSD_GROUPS], (0, 2, 1))
    a_neg = -jnp.exp(a_log.astype(F32))
    ys = []
    for d in range(2):
        dt_d = dt_pre[:, d * n_heads:(d + 1) * n_heads].reshape(tp, SSD_GROUPS, SSD_HEADS_PER_GROUP)
        dtc = jnp.pad(jnp.transpose(dt_d, (1, 0, 2)), ((0, 0), (0, 0), (0, LANES - SSD_HEADS_PER_GROUP)))
        dtr = jnp.transpose(dt_d, (1, 2, 0))
        ys.append(ssd_scan(xs, bc, bt, dtc, dtr, dt_bias[d], a_neg[d], d_skip[d], lay, reverse=(d == 1)))
    return ssd_gate(ys[0], ys[1], z, norm_w)


LRU_TB = 128


def _lru_scan_kernel(u_ref, gw_ref, gb_ref, lam_ref, o_ref, carry, *, reverse, n_valid_chunks):
    k = pl.program_id(0)
    tb = LRU_TB
    bw = u_ref.shape[1] // LRU_BLOCKS

    @pl.when(k == 0)
    def _():
        carry[...] = jnp.zeros_like(carry)

    @pl.when(k >= n_valid_chunks)
    def _():
        o_ref[...] = jnp.zeros_like(o_ref)

    @pl.when(k < n_valid_chunks)
    def _():
        u = u_ref[...]
        ub = u.astype(BF16)

        def pre(z):
            parts = [_dot(ub[:, n * bw:(n + 1) * bw], gw_ref[z, n]) for n in range(LRU_BLOCKS)]
            return jnp.concatenate(parts, axis=1) + gb_ref[z]

        r = jax.nn.sigmoid(pre(0))
        gi = jax.nn.sigmoid(pre(1))
        log_a = (-LRU_C) * r * _softplus(-lam_ref[...])
        a = jnp.exp(log_a)
        th = jnp.tanh(log_a)
        b = jnp.sqrt(-2.0 * th / (1.0 - th)) * (gi * u)
        row = lax.broadcasted_iota(jnp.int32, (tb, 1), 0)
        sh = 1
        while sh < tb:
            if reverse:
                keep = row < tb - sh
                a_s = jnp.where(keep, pltpu.roll(a, tb - sh, 0), 1.0)
                b_s = jnp.where(keep, pltpu.roll(b, tb - sh, 0), 0.0)
            else:
                keep = row >= sh
                a_s = jnp.where(keep, pltpu.roll(a, sh, 0), 1.0)
                b_s = jnp.where(keep, pltpu.roll(b, sh, 0), 0.0)
            b = a * b_s + b
            a = a * a_s
            sh *= 2
        hcur = b + a * carry[0:1, :]
        o_ref[...] = hcur
        carry[0:1, :] = hcur[0:1, :] if reverse else hcur[tb - 1:tb, :]


def lru_scan(u, gate_w, gate_b, lam, lay, reverse):
    tp, w = u.shape
    tb = LRU_TB
    bw = w // LRU_BLOCKS
    cidx = functools.partial(_chunk_index, lay=lay, reverse=reverse, chunk=tb)
    return pl.pallas_call(
        functools.partial(_lru_scan_kernel, reverse=reverse, n_valid_chunks=lay.n_valid // tb),
        grid=(tp // tb,),
        in_specs=[pl.BlockSpec((tb, w), lambda k: (cidx(k), 0)),
                  pl.BlockSpec((2, LRU_BLOCKS, bw, bw), lambda k: (0, 0, 0, 0)),
                  pl.BlockSpec((2, 1, w), lambda k: (0, 0, 0)),
                  pl.BlockSpec((1, w), lambda k: (0, 0))],
        out_specs=pl.BlockSpec((tb, w), lambda k: (cidx(k), 0)),
        out_shape=jax.ShapeDtypeStruct((tp, w), F32),
        scratch_shapes=[pltpu.VMEM((8, w), F32)],
        compiler_params=_cp(("arbitrary",)),
        name="lru_scan_rev" if reverse else "lru_scan_fwd",
    )(u, gate_w.astype(BF16), gate_b.reshape(2, 1, w), lam.reshape(1, w))


def _lru_gate_kernel(hf_ref, hb_ref, gp_ref, a_ref):
    a_ref[...] = ((hf_ref[...] + hb_ref[...]) * jax.nn.gelu(gp_ref[...], approximate=True)).astype(BF16)


def lru_gate(hf, hb, gate_pre):
    tp, w = hf.shape
    tm = TE
    spec = pl.BlockSpec((tm, w), lambda i: (i, 0))
    return pl.pallas_call(
        _lru_gate_kernel,
        grid=(tp // tm,),
        in_specs=[spec, spec, spec],
        out_specs=spec,
        out_shape=jax.ShapeDtypeStruct((tp, w), BF16),
        compiler_params=_cp(("parallel",)),
        name="lru_gate",
    )(hf, hb, gate_pre)


def lru_mixer(h, lay, w_in, conv_w, conv_b, gate_w, gate_b, lam):
    w = conv_w.shape[1]
    w16 = w_in.astype(BF16)
    gate_pre = mm(h, w16[:, :w], tn=512, name="lru_gate_proj")
    u = mm_conv(h, w16[:, w:], conv_w, conv_b, lay, tn=512, out_dtype=F32, name="lru_x")
    hs = [lru_scan(u, gate_w[d], gate_b[d], lam[d], lay, reverse=(d == 1)) for d in range(2)]
    return lru_gate(hs[0], hs[1], gate_pre)


def rope_tables(lay):
    n = lay.n_lat
    row = jnp.repeat(jnp.arange(n // GRID_W, dtype=F32), GRID_W)
    col = jnp.tile(jnp.arange(GRID_W, dtype=F32), n // GRID_W)
    n_freq = ROPE_DIM // 4
    inv = ROPE_THETA ** (-jnp.arange(n_freq, dtype=F32) / n_freq)
    ang = jnp.concatenate([row[:, None] * inv, col[:, None] * inv], axis=-1)
    cos, sin = jnp.cos(ang), jnp.sin(ang)
    zero = jnp.zeros_like(sin)
    c = jnp.tile(cos, (1, 4))
    s1 = jnp.tile(jnp.concatenate([-sin, zero], axis=1), (1, 2))
    s2 = jnp.tile(jnp.concatenate([zero, sin], axis=1), (1, 2))
    pad = lay.tp - n
    c = jnp.concatenate([c, jnp.ones((pad, LANES), F32)], axis=0)
    s1 = jnp.pad(s1, ((0, pad), (0, 0)))
    s2 = jnp.pad(s2, ((0, pad), (0, 0)))
    return c, s1, s2


MLA_QK_PAD = 256


def _mla_q_kernel(a_ref, g_ref, w_ref, c_ref, s1_ref, s2_ref, qr_ref, qp_ref, nscr, *, scale):
    @pl.when(pl.program_id(1) == 0)
    def _():
        x = a_ref[...]
        nscr[...] = (x * lax.rsqrt(jnp.mean(x * x, axis=-1, keepdims=True) + NORM_EPS) * g_ref[...]).astype(BF16)

    acc = _dot(nscr[...], w_ref[...]) * scale
    qp_ref[...] = acc.astype(BF16)
    rot = _rope128(acc[:, LANES:], c_ref[...], s1_ref[...], s2_ref[...])
    qr_ref[...] = jnp.concatenate([acc[:, :LANES], rot], axis=1).astype(BF16)


def _mla_kv_kernel(a_ref, kr_ref, g_ref, w_ref, c_ref, s1_ref, s2_ref, k_ref, v_ref, nscr, krscr):
    @pl.when(pl.program_id(1) == 0)
    def _():
        x = a_ref[...]
        nscr[...] = (x * lax.rsqrt(jnp.mean(x * x, axis=-1, keepdims=True) + NORM_EPS) * g_ref[...]).astype(BF16)
        krscr[...] = _rope128(kr_ref[...], c_ref[...], s1_ref[...], s2_ref[...]).astype(BF16)

    acc = _dot(nscr[...], w_ref[...])
    k_ref[...] = jnp.concatenate([acc[:, :LANES].astype(BF16), krscr[...]], axis=1)
    v_ref[...] = acc[:, LANES:].astype(BF16)


MLA_TQ = 512
MLA_TK = 512


def _mla_attn_kernel(qr_ref, qp_ref, k_ref, v_ref, o_ref, *, n_lat, n_ctx):
    qi = pl.program_id(1)
    tq, tk = MLA_TQ, MLA_TK

    def step(q, kk, vv, m, l, acc):
        s = _dot_nt(q, kk)
        m_new = jnp.maximum(m, jnp.max(s, axis=-1, keepdims=True))
        alpha = jnp.exp(m - m_new)
        p = jnp.exp(s - m_new)
        l = alpha * l + jnp.sum(p, axis=-1, keepdims=True)
        acc = alpha * acc + _dot(p.astype(BF16), vv)
        return m_new, l, acc

    m0 = jnp.full((tq, 1), -jnp.inf, F32)
    l0 = jnp.zeros((tq, 1), F32)
    acc0 = jnp.zeros((tq, MLA_V), F32)

    @pl.when(qi < n_lat // tq)
    def _():
        q = qr_ref[...]

        def body(c, carry):
            start = pl.multiple_of(c * tk, tk)
            return step(q, k_ref[pl.ds(start, tk), :], v_ref[pl.ds(start, tk), :], *carry)

        m, l, acc = lax.fori_loop(0, n_lat // tk, body, (m0, l0, acc0))
        m, l, acc = step(qp_ref[...], k_ref[pl.ds(n_lat, n_ctx), :], v_ref[pl.ds(n_lat, n_ctx), :], m, l, acc)
        o_ref[...] = (acc / l).astype(o_ref.dtype)

    @pl.when(qi >= n_lat // tq)
    def _():
        m, l, acc = step(qp_ref[...], k_ref[pl.ds(n_lat, n_ctx), :], v_ref[pl.ds(n_lat, n_ctx), :], m0, l0, acc0)
        o_ref[...] = (acc / l).astype(o_ref.dtype)


def mla_mixer(h, lay, tables, w_in, q_norm, kv_norm, w_q_up, w_kv_up):
    tp = h.shape[0]
    d = h.shape[1]
    c, s1, s2 = tables
    qk = MLA_NOPE + ROPE_DIM
    w_lat = jnp.concatenate([
        w_in[:, :MLA_Q_RANK], jnp.zeros((d, 256), F32),
        w_in[:, MLA_Q_RANK:MLA_Q_RANK + MLA_KV_RANK],
        w_in[:, MLA_Q_RANK + MLA_KV_RANK:], jnp.zeros((d, LANES - ROPE_DIM), F32)], axis=1).astype(BF16)
    lat = mm(h, w_lat, name="mla_lat")
    wq = jnp.pad(w_q_up.reshape(MLA_Q_RANK, MLA_HEADS, qk),
                 ((0, 0), (0, 0), (0, MLA_QK_PAD - qk))).reshape(MLA_Q_RANK, MLA_HEADS * MLA_QK_PAD).astype(BF16)
    n_t = tp // TM
    tab_spec = pl.BlockSpec((TM, LANES), lambda i, j: (i, 0))
    q_rot, q_plain = pl.pallas_call(
        functools.partial(_mla_q_kernel, scale=qk ** -0.5),
        grid=(n_t, MLA_HEADS),
        in_specs=[pl.BlockSpec((TM, MLA_Q_RANK), lambda i, j: (i, 0)),
                  pl.BlockSpec((1, MLA_Q_RANK), lambda i, j: (0, 0)),
                  pl.BlockSpec((MLA_Q_RANK, MLA_QK_PAD), lambda i, j: (0, j)),
                  tab_spec, tab_spec, tab_spec],
        out_specs=[pl.BlockSpec((None, TM, MLA_QK_PAD), lambda i, j: (j, i, 0)),
                   pl.BlockSpec((None, TM, MLA_QK_PAD), lambda i, j: (j, i, 0))],
        out_shape=[jax.ShapeDtypeStruct((MLA_HEADS, tp, MLA_QK_PAD), BF16)] * 2,
        scratch_shapes=[pltpu.VMEM((TM, MLA_Q_RANK), BF16)],
        compiler_params=_cp(("parallel", "arbitrary")),
        name="mla_q",
    )(lat, q_norm.reshape(1, -1), wq, c, s1, s2)
    kv_col = (MLA_Q_RANK + 256) // MLA_KV_RANK
    kr_col = (MLA_Q_RANK + 256 + MLA_KV_RANK) // LANES
    k_all, v_all = pl.pallas_call(
        _mla_kv_kernel,
        grid=(n_t, MLA_HEADS),
        in_specs=[pl.BlockSpec((TM, MLA_KV_RANK), lambda i, j: (i, kv_col)),
                  pl.BlockSpec((TM, LANES), lambda i, j: (i, kr_col)),
                  pl.BlockSpec((1, MLA_KV_RANK), lambda i, j: (0, 0)),
                  pl.BlockSpec((MLA_KV_RANK, MLA_NOPE + MLA_V), lambda i, j: (0, j)),
                  tab_spec, tab_spec, tab_spec],
        out_specs=[pl.BlockSpec((None, TM, MLA_QK_PAD), lambda i, j: (j, i, 0)),
                   pl.BlockSpec((None, TM, MLA_V), lambda i, j: (j, i, 0))],
        out_shape=[jax.ShapeDtypeStruct((MLA_HEADS, tp, MLA_QK_PAD), BF16),
                   jax.ShapeDtypeStruct((MLA_HEADS, tp, MLA_V), BF16)],
        scratch_shapes=[pltpu.VMEM((TM, MLA_KV_RANK), BF16), pltpu.VMEM((TM, LANES), BF16)],
        compiler_params=_cp(("parallel", "arbitrary")),
        name="mla_kv",
    )(lat, lat, kv_norm.reshape(1, -1), w_kv_up.astype(BF16), c, s1, s2)
    tq = MLA_TQ
    return pl.pallas_call(
        functools.partial(_mla_attn_kernel, n_lat=lay.n_lat, n_ctx=lay.n_ctx),
        grid=(MLA_HEADS, tp // tq),
        in_specs=[pl.BlockSpec((None, tq, MLA_QK_PAD), lambda hh, i: (hh, i, 0)),
                  pl.BlockSpec((None, tq, MLA_QK_PAD), lambda hh, i: (hh, i, 0)),
                  pl.BlockSpec((None, tp, MLA_QK_PAD), lambda hh, i: (hh, 0, 0)),
                  pl.BlockSpec((None, tp, MLA_V), lambda hh, i: (hh, 0, 0))],
        out_specs=pl.BlockSpec((tq, MLA_V), lambda hh, i: (i, hh)),
        out_shape=jax.ShapeDtypeStruct((tp, MLA_HEADS * MLA_V), BF16),
        compiler_params=_cp(("parallel", "parallel")),
        name="mla_attn",
    )(q_rot, q_plain, k_all, v_all)


def _mm_bias_rope_kernel(*refs, scale, rope, emit_plain):
    a_ref, w_ref, b_ref = refs[:3]
    if rope:
        c_ref, s1_ref, s2_ref = refs[3:6]
        outs = refs[6:]
    else:
        outs = refs[3:]
    acc = (_dot(a_ref[...], w_ref[...]) + b_ref[...]) * scale
    if not rope:
        outs[0][...] = acc.astype(outs[0].dtype)
        return
    c, s1, s2 = c_ref[...], s1_ref[...], s2_ref[...]
    tn = acc.shape[1]
    for s in range(tn // LANES):
        cols = slice(s * LANES, (s + 1) * LANES)
        outs[0][:, cols] = _rope128(acc[:, cols], c, s1, s2).astype(outs[0].dtype)
    if emit_plain:
        outs[1][...] = acc.astype(outs[1].dtype)


def mm_bias_rope(a, w, b, tables, *, scale=1.0, rope=True, emit_plain=False, name="mm_bias_rope"):
    m, k = a.shape
    n = w.shape[1]
    tn = 512
    in_specs = [pl.BlockSpec((TM, k), lambda i, j: (i, 0)),
                pl.BlockSpec((k, tn), lambda i, j: (0, j)),
                pl.BlockSpec((1, tn), lambda i, j: (0, j))]
    args = [a, w, b.reshape(1, n)]
    if rope:
        in_specs += [pl.BlockSpec((TM, LANES), lambda i, j: (i, 0))] * 3
        args += list(tables)
    n_out = 2 if (rope and emit_plain) else 1
    out_spec = pl.BlockSpec((TM, tn), lambda i, j: (i, j))
    res = pl.pallas_call(
        functools.partial(_mm_bias_rope_kernel, scale=scale, rope=rope, emit_plain=emit_plain),
        grid=(m // TM, n // tn),
        in_specs=in_specs,
        out_specs=[out_spec] * n_out,
        out_shape=[jax.ShapeDtypeStruct((m, n), BF16)] * n_out,
        compiler_params=_cp(("parallel", "parallel")),
        name=name,
    )(*args)
    return res if n_out == 2 else res[0]


def _swa_attn_kernel(qr_ref, qp_ref, kp_ref, kc_ref, kn_ref, vp_ref, vc_ref, vn_ref,
                     kctx_ref, vctx_ref, sink_ref, o_ref, *, n_lat):
    b = pl.program_id(1)
    blk = ATTN_BLOCK
    nb = n_lat // blk

    @pl.when(b >= nb)
    def _():
        o_ref[...] = jnp.zeros_like(o_ref)

    @pl.when(b < nb)
    def _():
        lo_half = lax.broadcasted_iota(jnp.int32, (blk, LANES), 1) < SWA_HEAD_DIM

        def stack_heads(q2):
            parts = []
            for t in range(2):
                piece = q2[:, t * LANES:(t + 1) * LANES]
                zero = jnp.zeros_like(piece)
                parts.append(jnp.where(lo_half, piece, zero))
                parts.append(jnp.where(lo_half, zero, piece))
            return jnp.concatenate(parts, axis=0)

        qrs = stack_heads(qr_ref[...])
        qps = stack_heads(qp_ref[...])
        kwin = jnp.concatenate([kp_ref[...], kc_ref[...], kn_ref[...]], axis=0)
        vwin = jnp.concatenate([vp_ref[...], vc_ref[...], vn_ref[...]], axis=0)
        s_w = _dot_nt(qrs, kwin)
        s_c = _dot_nt(qps, kctx_ref[...])
        r = lax.broadcasted_iota(jnp.int32, s_w.shape, 0) & (blk - 1)
        cidx = lax.broadcasted_iota(jnp.int32, s_w.shape, 1)
        dlt = cidx - r
        kpos = b * blk - blk + cidx
        ok = (dlt >= blk - SWA_WINDOW) & (dlt <= blk + SWA_WINDOW) & (kpos >= 0) & (kpos < n_lat)
        s_w = jnp.where(ok, s_w, -jnp.inf)
        sk = jnp.concatenate([jnp.broadcast_to(sink_ref[g:g + 1, 0:1], (blk, 1)) for g in range(SWA_GROUP)], axis=0)
        m = jnp.maximum(jnp.maximum(jnp.max(s_w, axis=-1, keepdims=True),
                                    jnp.max(s_c, axis=-1, keepdims=True)), sk)
        p_w = jnp.exp(s_w - m)
        p_c = jnp.exp(s_c - m)
        denom = (jnp.sum(p_w, axis=-1, keepdims=True) + jnp.sum(p_c, axis=-1, keepdims=True)
                 + jnp.exp(sk - m))
        o = (_dot(p_w.astype(BF16), vwin) + _dot(p_c.astype(BF16), vctx_ref[...])) / denom
        o_ref[...] = jnp.concatenate(
            [jnp.where(lo_half, o[0:blk], o[blk:2 * blk]),
             jnp.where(lo_half, o[2 * blk:3 * blk], o[3 * blk:4 * blk])], axis=1).astype(o_ref.dtype)


def swa_mixer(h, lay, tables, w_qkv, b_qkv, sink):
    tp = h.shape[0]
    d = h.shape[1]
    hd = SWA_HEAD_DIM
    qd = SWA_Q_HEADS * hd
    kd = SWA_KV_HEADS * hd

    def dup_heads(wm):
        lead = wm.shape[:-1]
        w3 = wm.reshape(lead + (SWA_KV_HEADS, 1, hd))
        return jnp.broadcast_to(w3, lead + (SWA_KV_HEADS, 2, hd)).reshape(lead + (2 * kd,))

    w16 = w_qkv.astype(BF16)
    q_rot, q_plain = mm_bias_rope(h, w16[:, :qd], b_qkv[:qd], tables, scale=hd ** -0.5,
                                  emit_plain=True, name="swa_q")
    k_dup = mm_bias_rope(h, dup_heads(w16[:, qd:qd + kd]), dup_heads(b_qkv[qd:qd + kd]), tables, name="swa_k")
    v_dup = mm_bias_rope(h, dup_heads(w16[:, qd + kd:]), dup_heads(b_qkv[qd + kd:]), tables,
                         rope=False, name="swa_v")
    blk = ATTN_BLOCK
    nb = lay.n_lat // blk
    ctx_blk = lay.n_lat // lay.n_ctx
    sink_t = jnp.broadcast_to(jnp.pad(sink.astype(F32).reshape(SWA_KV_HEADS, SWA_GROUP),
                                      ((0, 0), (0, 8 - SWA_GROUP)))[:, :, None], (SWA_KV_HEADS, 8, LANES))
    qw = SWA_GROUP * hd
    q_spec = pl.BlockSpec((blk, qw), lambda hh, b: (b, hh))

    def win_spec(off):
        return pl.BlockSpec((blk, LANES), lambda hh, b: (jnp.clip(b + off, 0, nb - 1), hh))

    ctx_spec = pl.BlockSpec((lay.n_ctx, LANES), lambda hh, b: (ctx_blk, hh))
    return pl.pallas_call(
        functools.partial(_swa_attn_kernel, n_lat=lay.n_lat),
        grid=(SWA_KV_HEADS, tp // blk),
        in_specs=[q_spec, q_spec, win_spec(-1), win_spec(0), win_spec(1),
                  win_spec(-1), win_spec(0), win_spec(1), ctx_spec, ctx_spec,
                  pl.BlockSpec((None, 8, LANES), lambda hh, b: (hh, 0, 0))],
        out_specs=pl.BlockSpec((blk, qw), lambda hh, b: (b, hh)),
        out_shape=jax.ShapeDtypeStruct((tp, qd), BF16),
        compiler_params=_cp(("parallel", "parallel")),
        name="swa_attn",
    )(q_rot, q_plain, k_dup, k_dup, k_dup, v_dup, v_dup, v_dup, k_dup, v_dup, sink_t)


def conv_ffn_hidden(h, lay, w_up, conv_w, conv_b):
    hidden = w_up.shape[1] // 2
    return mm_conv(h, w_up.astype(BF16), conv_w, conv_b, lay, tn=512, out_dtype=BF16, gated=True,
                   ncols=hidden, name="ffn_up")


def kernel(x, c, ctx, c_ctx, ada_w, ada_b, norm_mix, norm_ffn, ffn_up, ffn_conv_w, ffn_conv_b, ffn_down, final_norm, ssd_in, ssd_conv_w, ssd_conv_b, ssd_dt_bias, ssd_a_log, ssd_d, ssd_norm, ssd_out, lru_in, lru_conv_w, lru_conv_b, lru_gate_w, lru_gate_b, lru_lambda, lru_out, mla_in, mla_q_norm, mla_kv_norm, mla_q_up, mla_kv_up, mla_out, swa_qkv, swa_qkv_b, swa_sink, swa_out):
    bsz, n_lat, d = x.shape
    n_ctx = ctx.shape[1]
    depth = ada_w.shape[0]
    assert bsz == 1 and depth == 4, "one sample, four layers (one of each mixer)"
    lay = make_layout(n_lat, n_ctx)
    xs = jnp.concatenate([x[0], ctx[0], jnp.zeros((lay.tp - lay.n_valid, d), x.dtype)], axis=0)
    cv = jnp.concatenate([c, c_ctx[None, :], jnp.zeros((6, d), F32)], axis=0)
    mods = adaln(cv, ada_w, ada_b)
    tables = rope_tables(lay)
    for i in range(depth):
        mod = jnp.pad(mods[i, :2].reshape(2, ADA_CHUNKS, d), ((0, 0), (0, 8 - ADA_CHUNKS), (0, 0)))
        h = norm_mod(xs, norm_mix[i], mod, 0, 1, lay)
        if i == 0:
            a = ssd_mixer(h, lay, ssd_in[0], ssd_conv_w[0], ssd_conv_b[0], ssd_dt_bias[0], ssd_a_log[0],
                          ssd_d[0], ssd_norm[0])
            w_o = ssd_out[0]
        elif i == 1:
            a = lru_mixer(h, lay, lru_in[0], lru_conv_w[0], lru_conv_b[0], lru_gate_w[0], lru_gate_b[0],
                          lru_lambda[0])
            w_o = lru_out[0]
        elif i == 2:
            a = mla_mixer(h, lay, tables, mla_in[0], mla_q_norm[0], mla_kv_norm[0], mla_q_up[0], mla_kv_up[0])
            w_o = mla_out[0]
        else:
            a = swa_mixer(h, lay, tables, swa_qkv[0], swa_qkv_b[0], swa_sink[0])
            w_o = swa_out[0]
        xs = mm_resid(a, w_o.astype(BF16), xs, mod, 2, lay, name="mix_out")
        h = norm_mod(xs, norm_ffn[i], mod, 3, 4, lay)
        hid = conv_ffn_hidden(h, lay, ffn_up[i], ffn_conv_w[i], ffn_conv_b[i])
        xs = mm_resid(hid, ffn_down[i].astype(BF16), xs, mod, 5, lay, name="ffn_down")
    return rmsnorm_rows(xs, final_norm, n_lat)[None]
```

```python
import functools
import math
from typing import NamedTuple

import jax
import jax.numpy as jnp
from jax import lax
from jax.experimental import pallas as pl
from jax.experimental.pallas import tpu as pltpu

F32 = jnp.float32
BF16 = jnp.bfloat16

GRID_W = 64
NORM_EPS = 1e-6
ADA_CHUNKS = 6
ROPE_THETA = 10000.0
ROPE_DIM = 64
SSD_HEAD_DIM = 64
SSD_GROUPS = 8
SSD_HEADS_PER_GROUP = 8
SSD_STATE = 128
SSD_CHUNK = 128
LRU_BLOCKS = 8
LRU_C = 8.0
MLA_HEADS = 16
MLA_Q_RANK = 768
MLA_KV_RANK = 512
MLA_NOPE = 128
MLA_V = 128
SWA_Q_HEADS = 32
SWA_KV_HEADS = 8
SWA_GROUP = 4
SWA_HEAD_DIM = 64
SWA_WINDOW = 128
ATTN_BLOCK = 128

LANES = 128
MXU_COLS = 256
BF16_ROWS = 16
TM = 512
TE = 256
RESID_ROW_TILES = 8
VMEM_LIMIT = 56 * 2 ** 20
LOG2E = math.log2(math.e)


class Layout(NamedTuple):
    n_lat: int
    n_ctx: int
    tp: int

    @property
    def n_valid(self):
        return self.n_lat + self.n_ctx


def make_layout(n_lat, n_ctx):
    assert n_lat % TM == 0 and n_ctx % SSD_CHUNK == 0
    ctx_pad = -(-n_ctx // TM) * TM
    return Layout(n_lat, n_ctx, n_lat + ctx_pad)


def _cp(sem):
    return pltpu.CompilerParams(dimension_semantics=sem, vmem_limit_bytes=VMEM_LIMIT)


def _silu(x):
    return x * jax.nn.sigmoid(x)


def _softplus(x):
    return jnp.maximum(x, 0.0) + jnp.log1p(jnp.exp(-jnp.abs(x)))


def _dot(a, b):
    return jnp.dot(a, b, preferred_element_type=F32)


def _dot_nt(a, b):
    return lax.dot_general(a, b, (((1,), (1,)), ((), ())), preferred_element_type=F32)


def _split3(x):
    hi = x.astype(BF16)
    r = x - hi.astype(F32)
    mid = r.astype(BF16)
    lo = (r - mid.astype(F32)).astype(BF16)
    return hi, mid, lo


def _rope128(v, c, s1, s2):
    return v * c + pltpu.roll(v, 96, 1) * s1 + pltpu.roll(v, 32, 1) * s2


def _adaln_kernel(cv_ref, w_ref, b_ref, o_ref):
    s = _silu(cv_ref[...]).astype(BF16)
    o_ref[...] = _dot(s, w_ref[...].astype(BF16)) + b_ref[...]


def adaln(cv, ada_w, ada_b):
    depth, d, n = ada_w.shape
    tn = 1024
    return pl.pallas_call(
        _adaln_kernel,
        grid=(depth, n // tn),
        in_specs=[pl.BlockSpec((8, d), lambda l, j: (0, 0)),
                  pl.BlockSpec((None, d, tn), lambda l, j: (l, 0, j)),
                  pl.BlockSpec((None, 1, tn), lambda l, j: (l, 0, j))],
        out_specs=pl.BlockSpec((None, 8, tn), lambda l, j: (l, 0, j)),
        out_shape=jax.ShapeDtypeStruct((depth, 8, n), F32),
        compiler_params=_cp(("parallel", "parallel")),
        name="adaln",
    )(cv, ada_w, ada_b.reshape(depth, 1, n))


def _norm_mod_kernel(x_ref, g_ref, mod_ref, h_ref, *, shift_idx, scale_idx, n_valid, tm):
    x = x_ref[...]
    y = x * lax.rsqrt(jnp.mean(x * x, axis=-1, keepdims=True) + NORM_EPS) * g_ref[...]
    h = y * (1.0 + mod_ref[scale_idx:scale_idx + 1, :]) + mod_ref[shift_idx:shift_idx + 1, :]
    row = pl.program_id(0) * tm + lax.broadcasted_iota(jnp.int32, (tm, 1), 0)
    h_ref[...] = jnp.where(row < n_valid, h, 0.0).astype(BF16)


def norm_mod(x, g, mod, shift_idx, scale_idx, lay):
    tp, d = x.shape
    tm = TE
    return pl.pallas_call(
        functools.partial(_norm_mod_kernel, shift_idx=shift_idx, scale_idx=scale_idx,
                          n_valid=lay.n_valid, tm=tm),
        grid=(tp // tm,),
        in_specs=[pl.BlockSpec((tm, d), lambda i: (i, 0)),
                  pl.BlockSpec((1, d), lambda i: (0, 0)),
                  pl.BlockSpec((None, 8, d), lambda i: ((i * tm >= lay.n_lat).astype(jnp.int32), 0, 0))],
        out_specs=pl.BlockSpec((tm, d), lambda i: (i, 0)),
        out_shape=jax.ShapeDtypeStruct((tp, d), BF16),
        compiler_params=_cp(("parallel",)),
        name="norm_mod",
    )(x, g.reshape(1, d), mod)


def _rmsnorm_kernel(x_ref, g_ref, o_ref):
    x = x_ref[...]
    o_ref[...] = x * lax.rsqrt(jnp.mean(x * x, axis=-1, keepdims=True) + NORM_EPS) * g_ref[...]


def rmsnorm_rows(x, g, rows):
    d = x.shape[1]
    tm = TE
    return pl.pallas_call(
        _rmsnorm_kernel,
        grid=(rows // tm,),
        in_specs=[pl.BlockSpec((tm, d), lambda i: (i, 0)), pl.BlockSpec((1, d), lambda i: (0, 0))],
        out_specs=pl.BlockSpec((tm, d), lambda i: (i, 0)),
        out_shape=jax.ShapeDtypeStruct((rows, d), F32),
        compiler_params=_cp(("parallel",)),
        name="final_norm",
    )(x, g.reshape(1, d))


def _mm_kernel(a_ref, w_ref, o_ref):
    o_ref[...] = _dot(a_ref[...], w_ref[...]).astype(o_ref.dtype)


def mm(a, w, out_dtype=F32, tn=None, name="mm"):
    m, k = a.shape
    n = w.shape[1]
    tn = n if tn is None else tn
    return pl.pallas_call(
        _mm_kernel,
        grid=(m // TM, n // tn),
        in_specs=[pl.BlockSpec((TM, k), lambda i, j: (i, 0)),
                  pl.BlockSpec((k, tn), lambda i, j: (0, j))],
        out_specs=pl.BlockSpec((TM, tn), lambda i, j: (i, j)),
        out_shape=jax.ShapeDtypeStruct((m, n), out_dtype),
        compiler_params=_cp(("parallel", "parallel")),
        name=name,
    )(a, w)


def _mm_resid_kernel(a_ref, w_ref, x_ref, mod_ref, o_ref, *, gate_idx, n_lat, tm):
    acc = _dot(a_ref[...], w_ref[...])
    row = pl.program_id(0) * tm + lax.broadcasted_iota(jnp.int32, (tm, 1), 0)
    gate = jnp.where(row < n_lat, mod_ref[0, gate_idx:gate_idx + 1, :], mod_ref[1, gate_idx:gate_idx + 1, :])
    o_ref[...] = x_ref[...] + gate * acc


def mm_resid(a, w, x, mod, gate_idx, lay, name="mm_resid"):
    m, k = a.shape
    n = w.shape[1]
    tn = 512
    tm = m // RESID_ROW_TILES
    assert m % RESID_ROW_TILES == 0 and tm % BF16_ROWS == 0
    return pl.pallas_call(
        functools.partial(_mm_resid_kernel, gate_idx=gate_idx, n_lat=lay.n_lat, tm=tm),
        grid=(RESID_ROW_TILES, n // tn),
        in_specs=[pl.BlockSpec((tm, k), lambda i, j: (i, 0)),
                  pl.BlockSpec((k, tn), lambda i, j: (0, j)),
                  pl.BlockSpec((tm, tn), lambda i, j: (i, j)),
                  pl.BlockSpec((2, 8, tn), lambda i, j: (0, 0, j))],
        out_specs=pl.BlockSpec((tm, tn), lambda i, j: (i, j)),
        out_shape=jax.ShapeDtypeStruct((m, n), F32),
        compiler_params=_cp(("parallel", "parallel")),
        name=name,
    )(a, w, x, mod)


HALO = BF16_ROWS
CONV_ROW_CHUNK = 128


def _mm_conv_kernel(*refs, ksize, gated, act, tm, nj, n_lat_tiles, n_tiles, gw):
    nw = 2 if gated else 1
    a_ref, ap_ref, an_ref = refs[:3]
    w_refs = refs[3:3 + nw]
    cw_refs = refs[3 + nw:3 + 2 * nw]
    cb_refs = refs[3 + 2 * nw:3 + 3 * nw]
    o_ref = refs[3 + 3 * nw]
    hbuf = refs[4 + 3 * nw]
    u_scrs = refs[5 + 3 * nw:]
    t = pl.program_id(0)
    tc = jnp.minimum(t, n_tiles * nj - 1)
    i = tc // nj
    tn = w_refs[0].shape[1]

    @pl.when(t == 0)
    def _():
        for u in u_scrs:
            u[...] = jnp.zeros_like(u)

    @pl.when(tc % nj == 0)
    def _():
        first = (i == 0) | (i == n_lat_tiles)
        last = (i == n_lat_tiles - 1) | (i == n_tiles - 1)
        prev = ap_ref[...]
        nxt = an_ref[...]
        hbuf[0:HALO, :] = jnp.where(first, jnp.zeros_like(prev), prev)
        hbuf[HALO:HALO + tm, :] = a_ref[...]
        hbuf[HALO + tm:, :] = jnp.where(last, jnp.zeros_like(nxt), nxt)

    left = (ksize - 1) // 2

    rc = CONV_ROW_CHUNK
    n_rc = tm // rc

    def body(u_write, u_read):
        for ci in range(n_rc):
            r0 = 0 if ci == 0 else HALO + ci * rc
            r1 = tm + 2 * HALO if ci == n_rc - 1 else HALO + (ci + 1) * rc
            for wi in range(nw):
                u_write[wi][r0:r1, :] = _dot(hbuf[r0:r1, :], w_refs[wi][...])

            rows = slice(ci * rc, (ci + 1) * rc)
            base = HALO - left + ci * rc
            for lb in range(tn // LANES):
                cols = slice(lb * LANES, (lb + 1) * LANES)

                def conv(wi):
                    acc = cb_refs[wi][:, cols] + cw_refs[wi][0:1, cols] * u_read[wi][pl.ds(base, rc), cols]
                    for kk in range(1, ksize):
                        acc = acc + cw_refs[wi][kk:kk + 1, cols] * u_read[wi][pl.ds(base + kk, rc), cols]
                    return acc

                if gated:
                    out = _silu(conv(0)) * conv(1)
                else:
                    out = conv(0)
                    if act == "silu":
                        out = _silu(out)
                out = out.astype(o_ref.dtype)
                if gw is None:
                    o_ref[rows, cols] = out
                else:
                    o_ref[(lb * LANES) // gw, rows, (lb * LANES) % gw:(lb * LANES) % gw + LANES] = out

    @pl.when(t % 2 == 0)
    def _():
        body(u_scrs[:nw], u_scrs[nw:])

    @pl.when(t % 2 == 1)
    def _():
        body(u_scrs[nw:], u_scrs[:nw])


def mm_conv(a, w, conv_w, conv_b, lay, *, tn, out_dtype, gated=False, act=None,
            gw=None, col0=0, ncols=None, name="mm_conv"):
    tp, k = a.shape
    ksize = conv_w.shape[0]
    ncols = ncols if ncols is not None else w.shape[1]
    nj = ncols // tn
    c0 = col0 // tn
    assert col0 % tn == 0 and ncols % tn == 0
    assert gw is None or tn % gw == 0
    n_tiles = tp // TM
    total = n_tiles * nj
    hb = TM // HALO

    def cur(t):
        tc = jnp.minimum(t, total - 1)
        return tc // nj, tc % nj

    def prv(t):
        tq = jnp.maximum(t - 1, 0)
        return tq // nj, tq % nj

    a_specs = [pl.BlockSpec((TM, k), lambda t: (cur(t)[0], 0)),
               pl.BlockSpec((HALO, k), lambda t: (jnp.maximum(cur(t)[0] * hb - 1, 0), 0)),
               pl.BlockSpec((HALO, k), lambda t: (jnp.minimum((cur(t)[0] + 1) * hb, tp // HALO - 1), 0))]
    offs = [c0, c0 + nj] if gated else [c0]
    w_specs = [pl.BlockSpec((k, tn), lambda t, o=o: (0, o + cur(t)[1])) for o in offs]
    cw_specs = [pl.BlockSpec((ksize, tn), lambda t, o=o: (0, o + prv(t)[1])) for o in offs]
    cb_specs = [pl.BlockSpec((1, tn), lambda t, o=o: (0, o + prv(t)[1])) for o in offs]
    nw = len(offs)
    if gw is not None:
        out_spec = pl.BlockSpec((tn // gw, TM, gw), lambda t: (prv(t)[1], prv(t)[0], 0))
        out_shape = jax.ShapeDtypeStruct((ncols // gw, tp, gw), out_dtype)
    else:
        out_spec = pl.BlockSpec((TM, tn), lambda t: prv(t))
        out_shape = jax.ShapeDtypeStruct((tp, ncols), out_dtype)
    cb2 = conv_b.reshape(1, -1)
    return pl.pallas_call(
        functools.partial(_mm_conv_kernel, ksize=ksize, gated=gated, act=act, tm=TM, nj=nj,
                          n_lat_tiles=lay.n_lat // TM, n_tiles=n_tiles, gw=gw),
        grid=(total + 1,),
        in_specs=a_specs + w_specs + cw_specs + cb_specs,
        out_specs=out_spec,
        out_shape=out_shape,
        scratch_shapes=[pltpu.VMEM((TM + 2 * HALO, k), BF16)]
        + [pltpu.VMEM((TM + 2 * HALO, tn), F32) for _ in range(2 * nw)],
        compiler_params=_cp(("arbitrary",)),
        name=name,
    )(a, a, a, *([w] * nw), *([conv_w] * nw), *([cb2] * nw))


def _chunk_index(k, lay, reverse, chunk):
    n_lc = lay.n_lat // chunk
    n_cc = lay.n_ctx // chunk
    if reverse:
        valid = n_lc + n_cc - 1 - k
    else:
        valid = jnp.where(k < n_cc, n_lc + k, k - n_cc)
    return jnp.where(k < n_lc + n_cc, valid, k)


def _ssd_scan_kernel(xs_ref, b_ref, c_ref, bt_ref, dtc_ref, dtr_ref, biasc_ref, biasr_ref,
                     ac_ref, ar_ref, dsk_ref, y_ref, h_scr, *, reverse, n_valid_chunks):
    k = pl.program_id(1)
    L = SSD_CHUNK

    @pl.when(k == 0)
    def _():
        h_scr[...] = jnp.zeros_like(h_scr)

    @pl.when(k >= n_valid_chunks)
    def _():
        y_ref[...] = jnp.zeros_like(y_ref)

    @pl.when(k < n_valid_chunks)
    def _():
        li = lax.broadcasted_iota(jnp.int32, (L, L), 0)
        si = lax.broadcasted_iota(jnp.int32, (L, L), 1)
        if reverse:
            mask = si >= li
            mask_t = li >= si
        else:
            mask = si <= li
            mask_t = li <= si
        tri = jnp.where(mask, 1.0, 0.0).astype(BF16)
        tri_t = jnp.where(mask_t, 1.0, 0.0).astype(BF16)

        dtc = _softplus(dtc_ref[...] + biasc_ref[...])
        adt_c = dtc * ac_ref[...]
        hi, mid, lo = _split3(adt_c)
        acs_c = _dot(tri, hi) + _dot(tri, mid) + _dot(tri, lo)
        dtr = _softplus(dtr_ref[...] + biasr_ref[...])
        adt_r = dtr * ar_ref[...]
        hi, mid, lo = _split3(adt_r)
        acs_r = _dot(hi, tri_t) + _dot(mid, tri_t) + _dot(lo, tri_t)
        tot_r = acs_r[:, 0:1] if reverse else acs_r[:, L - 1:L]
        wdt_r = jnp.exp(tot_r - acs_r) * dtr
        etot = jnp.exp(tot_r)
        ecol = jnp.exp(acs_c)

        cb16 = c_ref[...]
        bt16 = bt_ref[...]
        cbm = jnp.where(mask, _dot(cb16, bt16), 0.0)
        cf = cb16.astype(F32)
        btf = bt16.astype(F32)
        lo_half = lax.broadcasted_iota(jnp.int32, (L, LANES), 1) < SSD_HEAD_DIM

        for p in range(SSD_HEADS_PER_GROUP // 2):
            cols = slice(p * LANES, (p + 1) * LANES)
            xp = xs_ref[:, cols]
            xpb = xp.astype(BF16)
            hp = h_scr[p]
            hpb = hp.astype(BF16)
            ys, hs = [], []
            for e2 in range(2):
                e = 2 * p + e2
                col = acs_c[:, e:e + 1]
                row = acs_r[e:e + 1, :]
                ld = (cbm * jnp.exp(jnp.minimum(col - row, 0.0)) * dtr[e:e + 1, :]).astype(BF16)
                lo_m = (cf * ecol[:, e:e + 1]).astype(BF16)
                ls = (btf * wdt_r[e:e + 1, :]).astype(BF16)
                ys.append(_dot(ld, xpb) + _dot(lo_m, hpb))
                hs.append(_dot(ls, xpb) + etot[e:e + 1, :] * hp)
            y_ref[:, cols] = jnp.where(lo_half, ys[0], ys[1]) + dsk_ref[:, cols] * xp
            h_scr[p] = jnp.where(lo_half, hs[0], hs[1])


def ssd_scan(xs, bc, bt, dtc, dtr, bias, a_neg, dskip, lay, reverse):
    g, tp, gw = xs.shape
    L = SSD_CHUNK
    n_chunks = tp // L
    n_valid = lay.n_valid // L
    cidx = functools.partial(_chunk_index, lay=lay, reverse=reverse, chunk=L)
    hpg = SSD_HEADS_PER_GROUP

    def lane_pad(v):
        return jnp.pad(v.reshape(g, 1, hpg), ((0, 0), (0, 0), (0, LANES - hpg)))

    biasc, ac = lane_pad(bias), lane_pad(a_neg)
    biasr, ar = bias.reshape(g, hpg, 1), a_neg.reshape(g, hpg, 1)
    dsk = jnp.repeat(dskip, SSD_HEAD_DIM).reshape(g, 1, gw)
    return pl.pallas_call(
        functools.partial(_ssd_scan_kernel, reverse=reverse, n_valid_chunks=n_valid),
        grid=(g, n_chunks),
        in_specs=[pl.BlockSpec((None, L, gw), lambda gi, k: (gi, cidx(k), 0)),
                  pl.BlockSpec((None, L, SSD_STATE), lambda gi, k: (gi, cidx(k), 0)),
                  pl.BlockSpec((None, L, SSD_STATE), lambda gi, k: (g + gi, cidx(k), 0)),
                  pl.BlockSpec((None, SSD_STATE, L), lambda gi, k: (gi, 0, cidx(k))),
                  pl.BlockSpec((None, L, LANES), lambda gi, k: (gi, cidx(k), 0)),
                  pl.BlockSpec((None, hpg, L), lambda gi, k: (gi, 0, cidx(k))),
                  pl.BlockSpec((None, 1, LANES), lambda gi, k: (gi, 0, 0)),
                  pl.BlockSpec((None, hpg, 1), lambda gi, k: (gi, 0, 0)),
                  pl.BlockSpec((None, 1, LANES), lambda gi, k: (gi, 0, 0)),
                  pl.BlockSpec((None, hpg, 1), lambda gi, k: (gi, 0, 0)),
                  pl.BlockSpec((None, 1, gw), lambda gi, k: (gi, 0, 0))],
        out_specs=pl.BlockSpec((None, L, gw), lambda gi, k: (gi, cidx(k), 0)),
        out_shape=jax.ShapeDtypeStruct((g, tp, gw), F32),
        scratch_shapes=[pltpu.VMEM((hpg // 2, SSD_STATE, LANES), F32)],
        compiler_params=_cp(("parallel", "arbitrary")),
        name="ssd_scan_rev" if reverse else "ssd_scan_fwd",
    )(xs, bc, bc, bt, dtc, dtr, biasc, biasr, ac, ar, dsk)


def _ssd_gate_kernel(yf_ref, yb_ref, z_ref, nw_ref, a_ref):
    gw = yf_ref.shape[-1]
    for g in range(SSD_GROUPS):
        cols = slice(g * gw, (g + 1) * gw)
        gz = (yf_ref[g] + yb_ref[g]) * _silu(z_ref[:, cols])
        gn = gz * lax.rsqrt(jnp.mean(gz * gz, axis=-1, keepdims=True) + NORM_EPS)
        a_ref[:, cols] = (gn * nw_ref[:, cols]).astype(BF16)


def ssd_gate(yf, yb, z, norm_w):
    g, tp, gw = yf.shape
    d_inner = g * gw
    tm = TE
    return pl.pallas_call(
        _ssd_gate_kernel,
        grid=(tp // tm,),
        in_specs=[pl.BlockSpec((g, tm, gw), lambda i: (0, i, 0)),
                  pl.BlockSpec((g, tm, gw), lambda i: (0, i, 0)),
                  pl.BlockSpec((tm, d_inner), lambda i: (i, 0)),
                  pl.BlockSpec((1, d_inner), lambda i: (0, 0))],
        out_specs=pl.BlockSpec((tm, d_inner), lambda i: (i, 0)),
        out_shape=jax.ShapeDtypeStruct((tp, d_inner), BF16),
        compiler_params=_cp(("parallel",)),
        name="ssd_gate",
    )(yf, yb, z, norm_w.reshape(1, d_inner))


def ssd_mixer(h, lay, w_in, conv_w, conv_b, dt_bias, a_log, d_skip, norm_w):
    d_inner = SSD_GROUPS * SSD_HEADS_PER_GROUP * SSD_HEAD_DIM
    gn = SSD_GROUPS * SSD_STATE
    conv_dim = d_inner + 2 * gn
    n_heads = SSD_GROUPS * SSD_HEADS_PER_GROUP
    tp = h.shape[0]
    w16 = w_in.astype(BF16)
    w_rest = w16[:, d_inner:d_inner + conv_dim]
    z = mm(h, w16[:, :d_inner], tn=512, name="ssd_z")
    xs = mm_conv(h, w_rest, conv_w, conv_b, lay, tn=SSD_HEADS_PER_GROUP * SSD_HEAD_DIM, out_dtype=F32,
                 act="silu", gw=SSD_HEADS_PER_GROUP * SSD_HEAD_DIM, col0=0, ncols=d_inner, name="ssd_x")
    bc = mm_conv(h, w_rest, conv_w, conv_b, lay, tn=512, out_dtype=BF16, act="silu",
                 gw=SSD_STATE, col0=d_inner, ncols=2 * gn, name="ssd_bc")
    dt_pre = mm(h, w16[:, d_inner + conv_dim:], name="ssd_dt")
    bt = jnp.transpose(bc[:SSD_GROUPS], (0, 2, 1))
    a_neg = -jnp.exp(a_log.astype(F32))
    ys = []
    for d in range(2):
        dt_d = dt_pre[:, d * n_heads:(d + 1) * n_heads].reshape(tp, SSD_GROUPS, SSD_HEADS_PER_GROUP)
        dtc = jnp.pad(jnp.transpose(dt_d, (1, 0, 2)), ((0, 0), (0, 0), (0, LANES - SSD_HEADS_PER_GROUP)))
        dtr = jnp.transpose(dt_d, (1, 2, 0))
        ys.append(ssd_scan(xs, bc, bt, dtc, dtr, dt_bias[d], a_neg[d], d_skip[d], lay, reverse=(d == 1)))
    return ssd_gate(ys[0], ys[1], z, norm_w)


LRU_TB = 128


def _lru_scan_kernel(u_ref, gw_ref, gb_ref, lam_ref, o_ref, carry, *, reverse, n_valid_chunks):
    k = pl.program_id(0)
    tb = LRU_TB
    bw = u_ref.shape[1] // LRU_BLOCKS

    @pl.when(k == 0)
    def _():
        carry[...] = jnp.zeros_like(carry)

    @pl.when(k >= n_valid_chunks)
    def _():
        o_ref[...] = jnp.zeros_like(o_ref)

    @pl.when(k < n_valid_chunks)
    def _():
        u = u_ref[...]
        ub = u.astype(BF16)

        def pre(z):
            parts = [_dot(ub[:, n * bw:(n + 1) * bw], gw_ref[z, n]) for n in range(LRU_BLOCKS)]
            return jnp.concatenate(parts, axis=1) + gb_ref[z]

        r = jax.nn.sigmoid(pre(0))
        gi = jax.nn.sigmoid(pre(1))
        log_a = (-LRU_C) * r * _softplus(-lam_ref[...])
        a = jnp.exp(log_a)
        th = jnp.tanh(log_a)
        b = jnp.sqrt(-2.0 * th / (1.0 - th)) * (gi * u)
        row = lax.broadcasted_iota(jnp.int32, (tb, 1), 0)
        sh = 1
        while sh < tb:
            if reverse:
                keep = row < tb - sh
                a_s = jnp.where(keep, pltpu.roll(a, tb - sh, 0), 1.0)
                b_s = jnp.where(keep, pltpu.roll(b, tb - sh, 0), 0.0)
            else:
                keep = row >= sh
                a_s = jnp.where(keep, pltpu.roll(a, sh, 0), 1.0)
                b_s = jnp.where(keep, pltpu.roll(b, sh, 0), 0.0)
            b = a * b_s + b
            a = a * a_s
            sh *= 2
        hcur = b + a * carry[0:1, :]
        o_ref[...] = hcur
        carry[0:1, :] = hcur[0:1, :] if reverse else hcur[tb - 1:tb, :]


def lru_scan(u, gate_w, gate_b, lam, lay, reverse):
    tp, w = u.shape
    tb = LRU_TB
    bw = w // LRU_BLOCKS
    cidx = functools.partial(_chunk_index, lay=lay, reverse=reverse, chunk=tb)
    return pl.pallas_call(
        functools.partial(_lru_scan_kernel, reverse=reverse, n_valid_chunks=lay.n_valid // tb),
        grid=(tp // tb,),
        in_specs=[pl.BlockSpec((tb, w), lambda k: (cidx(k), 0)),
                  pl.BlockSpec((2, LRU_BLOCKS, bw, bw), lambda k: (0, 0, 0, 0)),
                  pl.BlockSpec((2, 1, w), lambda k: (0, 0, 0)),
                  pl.BlockSpec((1, w), lambda k: (0, 0))],
        out_specs=pl.BlockSpec((tb, w), lambda k: (cidx(k), 0)),
        out_shape=jax.ShapeDtypeStruct((tp, w), F32),
        scratch_shapes=[pltpu.VMEM((8, w), F32)],
        compiler_params=_cp(("arbitrary",)),
        name="lru_scan_rev" if reverse else "lru_scan_fwd",
    )(u, gate_w.astype(BF16), gate_b.reshape(2, 1, w), lam.reshape(1, w))


def _lru_gate_kernel(hf_ref, hb_ref, gp_ref, a_ref):
    a_ref[...] = ((hf_ref[...] + hb_ref[...]) * jax.nn.gelu(gp_ref[...], approximate=True)).astype(BF16)


def lru_gate(hf, hb, gate_pre):
    tp, w = hf.shape
    tm = TE
    spec = pl.BlockSpec((tm, w), lambda i: (i, 0))
    return pl.pallas_call(
        _lru_gate_kernel,
        grid=(tp // tm,),
        in_specs=[spec, spec, spec],
        out_specs=spec,
        out_shape=jax.ShapeDtypeStruct((tp, w), BF16),
        compiler_params=_cp(("parallel",)),
        name="lru_gate",
    )(hf, hb, gate_pre)


def lru_mixer(h, lay, w_in, conv_w, conv_b, gate_w, gate_b, lam):
    w = conv_w.shape[1]
    w16 = w_in.astype(BF16)
    gate_pre = mm(h, w16[:, :w], tn=512, name="lru_gate_proj")
    u = mm_conv(h, w16[:, w:], conv_w, conv_b, lay, tn=512, out_dtype=F32, name="lru_x")
    hs = [lru_scan(u, gate_w[d], gate_b[d], lam[d], lay, reverse=(d == 1)) for d in range(2)]
    return lru_gate(hs[0], hs[1], gate_pre)


def rope_tables(lay):
    n = lay.n_lat
    row = jnp.repeat(jnp.arange(n // GRID_W, dtype=F32), GRID_W)
    col = jnp.tile(jnp.arange(GRID_W, dtype=F32), n // GRID_W)
    n_freq = ROPE_DIM // 4
    inv = ROPE_THETA ** (-jnp.arange(n_freq, dtype=F32) / n_freq)
    ang = jnp.concatenate([row[:, None] * inv, col[:, None] * inv], axis=-1)
    cos, sin = jnp.cos(ang), jnp.sin(ang)
    zero = jnp.zeros_like(sin)
    c = jnp.tile(cos, (1, 4))
    s1 = jnp.tile(jnp.concatenate([-sin, zero], axis=1), (1, 2))
    s2 = jnp.tile(jnp.concatenate([zero, sin], axis=1), (1, 2))
    pad = lay.tp - n
    c = jnp.concatenate([c, jnp.ones((pad, LANES), F32)], axis=0)
    s1 = jnp.pad(s1, ((0, pad), (0, 0)))
    s2 = jnp.pad(s2, ((0, pad), (0, 0)))
    return c, s1, s2


MLA_QK_PAD = 256


def _mla_q_kernel(a_ref, g_ref, w_ref, c_ref, s1_ref, s2_ref, qr_ref, qp_ref, nscr, *, scale):
    @pl.when(pl.program_id(1) == 0)
    def _():
        x = a_ref[...]
        nscr[...] = (x * lax.rsqrt(jnp.mean(x * x, axis=-1, keepdims=True) + NORM_EPS) * g_ref[...]).astype(BF16)

    acc = _dot(nscr[...], w_ref[...]) * scale
    qp_ref[...] = acc.astype(BF16)
    rot = _rope128(acc[:, LANES:], c_ref[...], s1_ref[...], s2_ref[...])
    qr_ref[...] = jnp.concatenate([acc[:, :LANES], rot], axis=1).astype(BF16)


def _mla_kv_kernel(a_ref, kr_ref, g_ref, w_ref, c_ref, s1_ref, s2_ref, k_ref, v_ref, nscr, krscr):
    @pl.when(pl.program_id(1) == 0)
    def _():
        x = a_ref[...]
        nscr[...] = (x * lax.rsqrt(jnp.mean(x * x, axis=-1, keepdims=True) + NORM_EPS) * g_ref[...]).astype(BF16)
        krscr[...] = _rope128(kr_ref[...], c_ref[...], s1_ref[...], s2_ref[...]).astype(BF16)

    acc = _dot(nscr[...], w_ref[...])
    k_ref[...] = jnp.concatenate([acc[:, :LANES].astype(BF16), krscr[...]], axis=1)
    v_ref[...] = jnp.concatenate([acc[:, LANES:].astype(BF16), jnp.ones((acc.shape[0], LANES), BF16)], axis=1)


MLA_TQ = 512
MLA_TK = 512
MLA_SUB = 256


def _mla_attn_kernel(qr_ref, qp_ref, k_ref, v_ref, o_ref, acc_ref, s_ref, *, n_lat, n_ctx):
    qi = pl.program_id(1)
    tq, tk, sub = MLA_TQ, MLA_TK, MLA_SUB
    nsub = tq // sub
    n_kt = n_lat // tk
    assert n_kt % 2 == 0
    rows = [slice(c * sub, (c + 1) * sub) for c in range(nsub)]

    def qk_lat(c, tile):
        start = pl.multiple_of(tile * tk, tk)
        return _dot_nt(qr_ref[rows[c], :], k_ref[pl.ds(start, tk), :])

    def consume(c, s, vv, m):
        m_new = jnp.maximum(m, jnp.max(s, axis=-1, keepdims=True))
        alpha = jnp.exp2(m - m_new)
        p = jnp.exp2(s - m_new).astype(BF16)
        acc_ref[rows[c], :] = alpha * acc_ref[rows[c], :] + _dot(p, vv)
        return m_new

    def v_tile(tile):
        return v_ref[pl.ds(pl.multiple_of(tile * tk, tk), tk), :]

    def half(ms, slot, tile):
        vv = v_tile(tile)
        out = []
        for c in range(nsub):
            s = s_ref[slot, rows[c], :]
            s_ref[1 - slot, rows[c], :] = qk_lat(c, tile + 1)
            out.append(consume(c, s, vv, ms[c]))
        return tuple(out)

    def finish():
        acc = acc_ref[...]
        o_ref[...] = (acc[:, :MLA_V] / acc[:, MLA_V:]).astype(o_ref.dtype)

    acc_ref[...] = jnp.zeros_like(acc_ref)
    m0 = tuple(jnp.full((sub, 1), -jnp.inf, F32) for _ in range(nsub))
    k_ctx = lambda: k_ref[pl.ds(n_lat, n_ctx), :]
    v_ctx = lambda: v_ref[pl.ds(n_lat, n_ctx), :]

    @pl.when(qi < n_lat // tq)
    def _():
        for c in range(nsub):
            s_ref[0, rows[c], :] = qk_lat(c, 0)

        def pair(pi, ms):
            ms = half(ms, 0, 2 * pi)
            return half(ms, 1, 2 * pi + 1)

        ms = lax.fori_loop(0, n_kt // 2 - 1, pair, m0)
        ms = half(ms, 0, n_kt - 2)
        vv = v_tile(n_kt - 1)
        for c in range(nsub):
            s = s_ref[1, rows[c], :]
            s_c = _dot_nt(qp_ref[rows[c], :], k_ctx())
            m = consume(c, s, vv, ms[c])
            consume(c, s_c, v_ctx(), m)
        finish()

    @pl.when(qi >= n_lat // tq)
    def _():
        for c in range(nsub):
            consume(c, _dot_nt(qp_ref[rows[c], :], k_ctx()), v_ctx(), m0[c])
        finish()


def mla_mixer(h, lay, tables, w_in, q_norm, kv_norm, w_q_up, w_kv_up):
    tp = h.shape[0]
    d = h.shape[1]
    c, s1, s2 = tables
    qk = MLA_NOPE + ROPE_DIM
    w_lat = jnp.concatenate([
        w_in[:, :MLA_Q_RANK], jnp.zeros((d, 256), F32),
        w_in[:, MLA_Q_RANK:MLA_Q_RANK + MLA_KV_RANK],
        w_in[:, MLA_Q_RANK + MLA_KV_RANK:], jnp.zeros((d, LANES - ROPE_DIM), F32)], axis=1).astype(BF16)
    lat = mm(h, w_lat, name="mla_lat")
    wq = jnp.pad(w_q_up.reshape(MLA_Q_RANK, MLA_HEADS, qk),
                 ((0, 0), (0, 0), (0, MLA_QK_PAD - qk))).reshape(MLA_Q_RANK, MLA_HEADS * MLA_QK_PAD).astype(BF16)
    n_t = tp // TM
    tab_spec = pl.BlockSpec((TM, LANES), lambda i, j: (i, 0))
    q_rot, q_plain = pl.pallas_call(
        functools.partial(_mla_q_kernel, scale=qk ** -0.5 * LOG2E),
        grid=(n_t, MLA_HEADS),
        in_specs=[pl.BlockSpec((TM, MLA_Q_RANK), lambda i, j: (i, 0)),
                  pl.BlockSpec((1, MLA_Q_RANK), lambda i, j: (0, 0)),
                  pl.BlockSpec((MLA_Q_RANK, MLA_QK_PAD), lambda i, j: (0, j)),
                  tab_spec, tab_spec, tab_spec],
        out_specs=[pl.BlockSpec((None, TM, MLA_QK_PAD), lambda i, j: (j, i, 0)),
                   pl.BlockSpec((None, TM, MLA_QK_PAD), lambda i, j: (j, i, 0))],
        out_shape=[jax.ShapeDtypeStruct((MLA_HEADS, tp, MLA_QK_PAD), BF16)] * 2,
        scratch_shapes=[pltpu.VMEM((TM, MLA_Q_RANK), BF16)],
        compiler_params=_cp(("parallel", "arbitrary")),
        name="mla_q",
    )(lat, q_norm.reshape(1, -1), wq, c, s1, s2)
    kv_col = (MLA_Q_RANK + 256) // MLA_KV_RANK
    kr_col = (MLA_Q_RANK + 256 + MLA_KV_RANK) // LANES
    k_all, v_all = pl.pallas_call(
        _mla_kv_kernel,
        grid=(n_t, MLA_HEADS),
        in_specs=[pl.BlockSpec((TM, MLA_KV_RANK), lambda i, j: (i, kv_col)),
                  pl.BlockSpec((TM, LANES), lambda i, j: (i, kr_col)),
                  pl.BlockSpec((1, MLA_KV_RANK), lambda i, j: (0, 0)),
                  pl.BlockSpec((MLA_KV_RANK, MLA_NOPE + MLA_V), lambda i, j: (0, j)),
                  tab_spec, tab_spec, tab_spec],
        out_specs=[pl.BlockSpec((None, TM, MLA_QK_PAD), lambda i, j: (j, i, 0)),
                   pl.BlockSpec((None, TM, MLA_V + LANES), lambda i, j: (j, i, 0))],
        out_shape=[jax.ShapeDtypeStruct((MLA_HEADS, tp, MLA_QK_PAD), BF16),
                   jax.ShapeDtypeStruct((MLA_HEADS, tp, MLA_V + LANES), BF16)],
        scratch_shapes=[pltpu.VMEM((TM, MLA_KV_RANK), BF16), pltpu.VMEM((TM, LANES), BF16)],
        compiler_params=_cp(("parallel", "arbitrary")),
        name="mla_kv",
    )(lat, lat, kv_norm.reshape(1, -1), w_kv_up.astype(BF16), c, s1, s2)
    tq = MLA_TQ
    return pl.pallas_call(
        functools.partial(_mla_attn_kernel, n_lat=lay.n_lat, n_ctx=lay.n_ctx),
        grid=(MLA_HEADS, tp // tq),
        in_specs=[pl.BlockSpec((None, tq, MLA_QK_PAD), lambda hh, i: (hh, i, 0)),
                  pl.BlockSpec((None, tq, MLA_QK_PAD), lambda hh, i: (hh, i, 0)),
                  pl.BlockSpec((None, tp, MLA_QK_PAD), lambda hh, i: (hh, 0, 0)),
                  pl.BlockSpec((None, tp, MLA_V + LANES), lambda hh, i: (hh, 0, 0))],
        out_specs=pl.BlockSpec((tq, MLA_V), lambda hh, i: (i, hh)),
        out_shape=jax.ShapeDtypeStruct((tp, MLA_HEADS * MLA_V), BF16),
        scratch_shapes=[pltpu.VMEM((tq, MLA_V + LANES), F32), pltpu.VMEM((2, tq, MLA_TK), F32)],
        compiler_params=_cp(("parallel", "parallel")),
        name="mla_attn",
    )(q_rot, q_plain, k_all, v_all)


def _mm_bias_rope_kernel(*refs, scale, rope, emit_plain):
    a_ref, w_ref, b_ref = refs[:3]
    if rope:
        c_ref, s1_ref, s2_ref = refs[3:6]
        outs = refs[6:]
    else:
        outs = refs[3:]
    acc = (_dot(a_ref[...], w_ref[...]) + b_ref[...]) * scale
    if not rope:
        outs[0][...] = acc.astype(outs[0].dtype)
        return
    c, s1, s2 = c_ref[...], s1_ref[...], s2_ref[...]
    tn = acc.shape[1]
    for s in range(tn // LANES):
        cols = slice(s * LANES, (s + 1) * LANES)
        outs[0][:, cols] = _rope128(acc[:, cols], c, s1, s2).astype(outs[0].dtype)
    if emit_plain:
        outs[1][...] = acc.astype(outs[1].dtype)


def mm_bias_rope(a, w, b, tables, *, scale=1.0, rope=True, emit_plain=False, name="mm_bias_rope"):
    m, k = a.shape
    n = w.shape[1]
    tn = 512
    in_specs = [pl.BlockSpec((TM, k), lambda i, j: (i, 0)),
                pl.BlockSpec((k, tn), lambda i, j: (0, j)),
                pl.BlockSpec((1, tn), lambda i, j: (0, j))]
    args = [a, w, b.reshape(1, n)]
    if rope:
        in_specs += [pl.BlockSpec((TM, LANES), lambda i, j: (i, 0))] * 3
        args += list(tables)
    n_out = 2 if (rope and emit_plain) else 1
    out_spec = pl.BlockSpec((TM, tn), lambda i, j: (i, j))
    res = pl.pallas_call(
        functools.partial(_mm_bias_rope_kernel, scale=scale, rope=rope, emit_plain=emit_plain),
        grid=(m // TM, n // tn),
        in_specs=in_specs,
        out_specs=[out_spec] * n_out,
        out_shape=[jax.ShapeDtypeStruct((m, n), BF16)] * n_out,
        compiler_params=_cp(("parallel", "parallel")),
        name=name,
    )(*args)
    return res if n_out == 2 else res[0]


SWA_BPS = 2


def _swa_attn_kernel(qr_ref, qp_ref, kp_ref, kc_ref, kn_ref, vp_ref, vc_ref, vn_ref,
                     kctx_ref, vctx_ref, sink_ref, o_ref, *, n_lat):
    b = pl.program_id(1)
    blk = ATTN_BLOCK
    nb = n_lat // blk

    @pl.when(b * SWA_BPS >= nb)
    def _():
        o_ref[...] = jnp.zeros_like(o_ref)

    @pl.when(b * SWA_BPS < nb)
    def _():
        lane = lax.broadcasted_iota(jnp.int32, (blk, LANES), 1)
        r = lax.broadcasted_iota(jnp.int32, (blk, blk), 0)
        c = lax.broadcasted_iota(jnp.int32, (blk, blk), 1)
        lo_half = lane < SWA_HEAD_DIM
        kwin = jnp.concatenate([kp_ref[...], kc_ref[...], kn_ref[...]], axis=0)
        vwin = jnp.concatenate([vp_ref[...], vc_ref[...], vn_ref[...]], axis=0)
        ones_w = jnp.ones(vwin.shape, BF16)
        vwe = jnp.concatenate([vwin, ones_w], axis=1)
        vce = jnp.concatenate([vctx_ref[...], ones_w[:vctx_ref.shape[0]]], axis=1)
        kctx = kctx_ref[...]
        n_c = kctx.shape[0]
        scores = []
        for t in range(SWA_BPS):
            gb = b * SWA_BPS + t
            rows = slice(t * blk, (t + 1) * blk)
            win = slice(t * blk, (t + 3) * blk)
            bias_p = jnp.where((c >= r) & (gb > 0), 0.0, -jnp.inf)
            bias_n = jnp.where((c <= r) & (gb < nb - 1), 0.0, -jnp.inf)
            for g in range(SWA_GROUP):
                cols = slice((g // 2) * LANES, (g // 2 + 1) * LANES)
                keep = lo_half if g % 2 == 0 else jnp.logical_not(lo_half)
                qr = qr_ref[rows, cols]
                qp = qp_ref[rows, cols]
                qr = jnp.where(keep, qr, jnp.zeros_like(qr))
                qp = jnp.where(keep, qp, jnp.zeros_like(qp))
                s_c = _dot_nt(qp, kctx)
                s_w = _dot_nt(qr, kwin[win])
                scores.append(jnp.concatenate([s_c, s_w[:, :blk] + bias_p, s_w[:, blk:2 * blk],
                                               s_w[:, 2 * blk:] + bias_n], axis=1))
        for t in range(SWA_BPS):
            rows = slice(t * blk, (t + 1) * blk)
            win = slice(t * blk, (t + 3) * blk)
            outs = []
            for g in range(SWA_GROUP):
                s = scores[t * SWA_GROUP + g]
                sk = sink_ref[g:g + 1, 0:1] * LOG2E
                m = jnp.maximum(jnp.max(s, axis=-1, keepdims=True), sk)
                p = jnp.exp2(s - m).astype(BF16)
                oe = _dot(p[:, :n_c], vce) + _dot(p[:, n_c:], vwe[win])
                outs.append(oe[:, :LANES] / (oe[:, LANES:] + jnp.exp2(sk - m)))
            o_ref[rows, :] = jnp.concatenate(
                [jnp.where(lo_half, outs[0], outs[1]), jnp.where(lo_half, outs[2], outs[3])],
                axis=1).astype(o_ref.dtype)


def swa_mixer(h, lay, tables, w_qkv, b_qkv, sink):
    tp = h.shape[0]
    d = h.shape[1]
    hd = SWA_HEAD_DIM
    qd = SWA_Q_HEADS * hd
    kd = SWA_KV_HEADS * hd

    def dup_heads(wm):
        lead = wm.shape[:-1]
        w3 = wm.reshape(lead + (SWA_KV_HEADS, 1, hd))
        return jnp.broadcast_to(w3, lead + (SWA_KV_HEADS, 2, hd)).reshape(lead + (2 * kd,))

    w16 = w_qkv.astype(BF16)
    q_rot, q_plain = mm_bias_rope(h, w16[:, :qd], b_qkv[:qd], tables, scale=hd ** -0.5 * LOG2E,
                                  emit_plain=True, name="swa_q")
    k_dup = mm_bias_rope(h, dup_heads(w16[:, qd:qd + kd]), dup_heads(b_qkv[qd:qd + kd]), tables, name="swa_k")
    v_dup = mm_bias_rope(h, dup_heads(w16[:, qd + kd:]), dup_heads(b_qkv[qd + kd:]), tables,
                         rope=False, name="swa_v")
    blk = ATTN_BLOCK
    nb = lay.n_lat // blk
    ctx_blk = lay.n_lat // lay.n_ctx
    sink_t = jnp.broadcast_to(jnp.pad(sink.astype(F32).reshape(SWA_KV_HEADS, SWA_GROUP),
                                      ((0, 0), (0, 8 - SWA_GROUP)))[:, :, None], (SWA_KV_HEADS, 8, LANES))
    qw = SWA_GROUP * hd
    bps = SWA_BPS
    assert nb % bps == 0 and tp % (blk * bps) == 0
    q_spec = pl.BlockSpec((blk * bps, qw), lambda hh, b: (b, hh))
    prev_spec = pl.BlockSpec((blk, LANES), lambda hh, b: (jnp.clip(b * bps - 1, 0, nb - 1), hh))
    cur_spec = pl.BlockSpec((blk * bps, LANES), lambda hh, b: (jnp.minimum(b, nb // bps - 1), hh))
    next_spec = pl.BlockSpec((blk, LANES), lambda hh, b: (jnp.clip(b * bps + bps, 0, nb - 1), hh))
    ctx_spec = pl.BlockSpec((lay.n_ctx, LANES), lambda hh, b: (ctx_blk, hh))
    return pl.pallas_call(
        functools.partial(_swa_attn_kernel, n_lat=lay.n_lat),
        grid=(SWA_KV_HEADS, tp // (blk * bps)),
        in_specs=[q_spec, q_spec, prev_spec, cur_spec, next_spec,
                  prev_spec, cur_spec, next_spec, ctx_spec, ctx_spec,
                  pl.BlockSpec((None, 8, LANES), lambda hh, b: (hh, 0, 0))],
        out_specs=pl.BlockSpec((blk * bps, qw), lambda hh, b: (b, hh)),
        out_shape=jax.ShapeDtypeStruct((tp, qd), BF16),
        compiler_params=_cp(("parallel", "parallel")),
        name="swa_attn",
    )(q_rot, q_plain, k_dup, k_dup, k_dup, v_dup, v_dup, v_dup, k_dup, v_dup, sink_t)


def conv_ffn_hidden(h, lay, w_up, conv_w, conv_b):
    hidden = w_up.shape[1] // 2
    return mm_conv(h, w_up.astype(BF16), conv_w, conv_b, lay, tn=512, out_dtype=BF16, gated=True,
                   ncols=hidden, name="ffn_up")


def kernel(x, c, ctx, c_ctx, ada_w, ada_b, norm_mix, norm_ffn, ffn_up, ffn_conv_w, ffn_conv_b, ffn_down, final_norm, ssd_in, ssd_conv_w, ssd_conv_b, ssd_dt_bias, ssd_a_log, ssd_d, ssd_norm, ssd_out, lru_in, lru_conv_w, lru_conv_b, lru_gate_w, lru_gate_b, lru_lambda, lru_out, mla_in, mla_q_norm, mla_kv_norm, mla_q_up, mla_kv_up, mla_out, swa_qkv, swa_qkv_b, swa_sink, swa_out):
    bsz, n_lat, d = x.shape
    n_ctx = ctx.shape[1]
    depth = ada_w.shape[0]
    assert bsz == 1 and depth == 4, "one sample, four layers (one of each mixer)"
    lay = make_layout(n_lat, n_ctx)
    xs = jnp.concatenate([x[0], ctx[0], jnp.zeros((lay.tp - lay.n_valid, d), x.dtype)], axis=0)
    cv = jnp.concatenate([c, c_ctx[None, :], jnp.zeros((6, d), F32)], axis=0)
    mods = adaln(cv, ada_w, ada_b)
    tables = rope_tables(lay)
    for i in range(depth):
        mod = jnp.pad(mods[i, :2].reshape(2, ADA_CHUNKS, d), ((0, 0), (0, 8 - ADA_CHUNKS), (0, 0)))
        h = norm_mod(xs, norm_mix[i], mod, 0, 1, lay)
        if i == 0:
            a = ssd_mixer(h, lay, ssd_in[0], ssd_conv_w[0], ssd_conv_b[0], ssd_dt_bias[0], ssd_a_log[0],
                          ssd_d[0], ssd_norm[0])
            w_o = ssd_out[0]
        elif i == 1:
            a = lru_mixer(h, lay, lru_in[0], lru_conv_w[0], lru_conv_b[0], lru_gate_w[0], lru_gate_b[0],
                          lru_lambda[0])
            w_o = lru_out[0]
        elif i == 2:
            a = mla_mixer(h, lay, tables, mla_in[0], mla_q_norm[0], mla_kv_norm[0], mla_q_up[0], mla_kv_up[0])
            w_o = mla_out[0]
        else:
            a = swa_mixer(h, lay, tables, swa_qkv[0], swa_qkv_b[0], swa_sink[0])
            w_o = swa_out[0]
        xs = mm_resid(a, w_o.astype(BF16), xs, mod, 2, lay, name="mix_out")
        h = norm_mod(xs, norm_ffn[i], mod, 3, 4, lay)
        hid = conv_ffn_hidden(h, lay, ffn_up[i], ffn_conv_w[i], ffn_conv_b[i])
        xs = mm_resid(hid, ffn_down[i].astype(BF16), xs, mod, 5, lay, name="ffn_down")
    return rmsnorm_rows(xs, final_norm, n_lat)[None]
```

```python
import functools
import math
from typing import NamedTuple

import jax
import jax.numpy as jnp
from jax import lax
from jax.experimental import pallas as pl
from jax.experimental.pallas import tpu as pltpu

F32 = jnp.float32
BF16 = jnp.bfloat16

GRID_W = 64
NORM_EPS = 1e-6
ADA_CHUNKS = 6
ROPE_THETA = 10000.0
ROPE_DIM = 64
SSD_HEAD_DIM = 64
SSD_GROUPS = 8
SSD_HEADS_PER_GROUP = 8
SSD_STATE = 128
SSD_CHUNK = 128
LRU_BLOCKS = 8
LRU_C = 8.0
MLA_HEADS = 16
MLA_Q_RANK = 768
MLA_KV_RANK = 512
MLA_NOPE = 128
MLA_V = 128
SWA_Q_HEADS = 32
SWA_KV_HEADS = 8
SWA_GROUP = 4
SWA_HEAD_DIM = 64
SWA_WINDOW = 128
ATTN_BLOCK = 128

LANES = 128
SUBLANES = 8
MXU_COLS = 256
BF16_ROWS = 16
TM = 512
TE = 256
RESID_ROW_TILES = 8
VMEM_LIMIT = 56 * 2 ** 20
LOG2E = math.log2(math.e)


class Layout(NamedTuple):
    n_lat: int
    n_ctx: int
    tp: int

    @property
    def n_valid(self):
        return self.n_lat + self.n_ctx


def make_layout(n_lat, n_ctx):
    assert n_lat % TM == 0 and n_ctx % SSD_CHUNK == 0
    ctx_pad = -(-n_ctx // TM) * TM
    return Layout(n_lat, n_ctx, n_lat + ctx_pad)


def _cp(sem):
    return pltpu.CompilerParams(dimension_semantics=sem, vmem_limit_bytes=VMEM_LIMIT)


def _silu(x):
    return x * jax.nn.sigmoid(x)


def _softplus(x):
    return jnp.maximum(x, 0.0) + jnp.log1p(jnp.exp(-jnp.abs(x)))


def _dot(a, b):
    return jnp.dot(a, b, preferred_element_type=F32)


def _dot_nt(a, b):
    return lax.dot_general(a, b, (((1,), (1,)), ((), ())), preferred_element_type=F32)


def _split3(x):
    hi = x.astype(BF16)
    r = x - hi.astype(F32)
    mid = r.astype(BF16)
    lo = (r - mid.astype(F32)).astype(BF16)
    return hi, mid, lo


def _rope128(v, c, s1, s2):
    return v * c + pltpu.roll(v, 96, 1) * s1 + pltpu.roll(v, 32, 1) * s2


def _adaln_kernel(cv_ref, w_ref, b_ref, o_ref):
    s = _silu(cv_ref[...]).astype(BF16)
    o_ref[...] = _dot(s, w_ref[...].astype(BF16)) + b_ref[...]


def adaln(cv, ada_w, ada_b):
    depth, d, n = ada_w.shape
    tn = 1024
    return pl.pallas_call(
        _adaln_kernel,
        grid=(depth, n // tn),
        in_specs=[pl.BlockSpec((8, d), lambda l, j: (0, 0)),
                  pl.BlockSpec((None, d, tn), lambda l, j: (l, 0, j)),
                  pl.BlockSpec((None, 1, tn), lambda l, j: (l, 0, j))],
        out_specs=pl.BlockSpec((None, 8, tn), lambda l, j: (l, 0, j)),
        out_shape=jax.ShapeDtypeStruct((depth, 8, n), F32),
        compiler_params=_cp(("parallel", "parallel")),
        name="adaln",
    )(cv, ada_w, ada_b.reshape(depth, 1, n))


def _norm_mod_kernel(x_ref, g_ref, mod_ref, h_ref, *, shift_idx, scale_idx, n_valid, tm):
    x = x_ref[...]
    y = x * lax.rsqrt(jnp.mean(x * x, axis=-1, keepdims=True) + NORM_EPS) * g_ref[...]
    h = y * (1.0 + mod_ref[scale_idx:scale_idx + 1, :]) + mod_ref[shift_idx:shift_idx + 1, :]
    row = pl.program_id(0) * tm + lax.broadcasted_iota(jnp.int32, (tm, 1), 0)
    h_ref[...] = jnp.where(row < n_valid, h, 0.0).astype(BF16)


def norm_mod(x, g, mod, shift_idx, scale_idx, lay):
    tp, d = x.shape
    tm = TE
    return pl.pallas_call(
        functools.partial(_norm_mod_kernel, shift_idx=shift_idx, scale_idx=scale_idx,
                          n_valid=lay.n_valid, tm=tm),
        grid=(tp // tm,),
        in_specs=[pl.BlockSpec((tm, d), lambda i: (i, 0)),
                  pl.BlockSpec((1, d), lambda i: (0, 0)),
                  pl.BlockSpec((None, 8, d), lambda i: ((i * tm >= lay.n_lat).astype(jnp.int32), 0, 0))],
        out_specs=pl.BlockSpec((tm, d), lambda i: (i, 0)),
        out_shape=jax.ShapeDtypeStruct((tp, d), BF16),
        compiler_params=_cp(("parallel",)),
        name="norm_mod",
    )(x, g.reshape(1, d), mod)


def _rmsnorm_kernel(x_ref, g_ref, o_ref):
    x = x_ref[...]
    o_ref[...] = x * lax.rsqrt(jnp.mean(x * x, axis=-1, keepdims=True) + NORM_EPS) * g_ref[...]


def rmsnorm_rows(x, g, rows):
    d = x.shape[1]
    tm = TE
    return pl.pallas_call(
        _rmsnorm_kernel,
        grid=(rows // tm,),
        in_specs=[pl.BlockSpec((tm, d), lambda i: (i, 0)), pl.BlockSpec((1, d), lambda i: (0, 0))],
        out_specs=pl.BlockSpec((tm, d), lambda i: (i, 0)),
        out_shape=jax.ShapeDtypeStruct((rows, d), F32),
        compiler_params=_cp(("parallel",)),
        name="final_norm",
    )(x, g.reshape(1, d))


def _mm_kernel(a_ref, w_ref, o_ref):
    o_ref[...] = _dot(a_ref[...], w_ref[...]).astype(o_ref.dtype)


def mm(a, w, out_dtype=F32, tn=None, name="mm"):
    m, k = a.shape
    n = w.shape[1]
    tn = n if tn is None else tn
    return pl.pallas_call(
        _mm_kernel,
        grid=(m // TM, n // tn),
        in_specs=[pl.BlockSpec((TM, k), lambda i, j: (i, 0)),
                  pl.BlockSpec((k, tn), lambda i, j: (0, j))],
        out_specs=pl.BlockSpec((TM, tn), lambda i, j: (i, j)),
        out_shape=jax.ShapeDtypeStruct((m, n), out_dtype),
        compiler_params=_cp(("parallel", "parallel")),
        name=name,
    )(a, w)


def _mm_resid_kernel(a_ref, w_ref, x_ref, mod_ref, o_ref, *, gate_idx, n_lat, tm):
    acc = _dot(a_ref[...], w_ref[...])
    row = pl.program_id(0) * tm + lax.broadcasted_iota(jnp.int32, (tm, 1), 0)
    gate = jnp.where(row < n_lat, mod_ref[0, gate_idx:gate_idx + 1, :], mod_ref[1, gate_idx:gate_idx + 1, :])
    o_ref[...] = x_ref[...] + gate * acc


def mm_resid(a, w, x, mod, gate_idx, lay, name="mm_resid"):
    m, k = a.shape
    n = w.shape[1]
    tn = 512
    tm = m // RESID_ROW_TILES
    assert m % RESID_ROW_TILES == 0 and tm % BF16_ROWS == 0
    return pl.pallas_call(
        functools.partial(_mm_resid_kernel, gate_idx=gate_idx, n_lat=lay.n_lat, tm=tm),
        grid=(RESID_ROW_TILES, n // tn),
        in_specs=[pl.BlockSpec((tm, k), lambda i, j: (i, 0)),
                  pl.BlockSpec((k, tn), lambda i, j: (0, j)),
                  pl.BlockSpec((tm, tn), lambda i, j: (i, j)),
                  pl.BlockSpec((2, 8, tn), lambda i, j: (0, 0, j))],
        out_specs=pl.BlockSpec((tm, tn), lambda i, j: (i, j)),
        out_shape=jax.ShapeDtypeStruct((m, n), F32),
        compiler_params=_cp(("parallel", "parallel")),
        name=name,
    )(a, w, x, mod)


HALO = BF16_ROWS


def _mm_conv_kernel(*refs, ksize, gated, act, tm, n_lat_tiles, n_tiles, gw):
    nw = 2 if gated else 1
    a_ref, ap_ref, an_ref = refs[:3]
    w_refs = refs[3:3 + nw]
    cw_refs = refs[3 + nw:3 + 2 * nw]
    cb_refs = refs[3 + 2 * nw:3 + 3 * nw]
    o_ref = refs[3 + 3 * nw]
    hbuf = refs[4 + 3 * nw]
    u_scrs = refs[5 + 3 * nw:]
    i = pl.program_id(0)
    tn = w_refs[0].shape[1]

    @pl.when(pl.program_id(1) == 0)
    def _():
        first = (i == 0) | (i == n_lat_tiles)
        last = (i == n_lat_tiles - 1) | (i == n_tiles - 1)
        prev = ap_ref[...]
        nxt = an_ref[...]
        hbuf[0:HALO, :] = jnp.where(first, jnp.zeros_like(prev), prev)
        hbuf[HALO:HALO + tm, :] = a_ref[...]
        hbuf[HALO + tm:, :] = jnp.where(last, jnp.zeros_like(nxt), nxt)

    left = (ksize - 1) // 2

    def conv(wi):
        u_scrs[wi][...] = _dot(hbuf[...], w_refs[wi][...])
        acc = cb_refs[wi][...] + cw_refs[wi][0:1, :] * u_scrs[wi][pl.ds(HALO - left, tm), :]
        for kk in range(1, ksize):
            acc = acc + cw_refs[wi][kk:kk + 1, :] * u_scrs[wi][pl.ds(HALO - left + kk, tm), :]
        return acc

    if gated:
        out = _silu(conv(0)) * conv(1)
    else:
        out = conv(0)
        if act == "silu":
            out = _silu(out)
    out = out.astype(o_ref.dtype)
    if gw is None:
        o_ref[...] = out
    else:
        for q in range(tn // gw):
            o_ref[q] = out[:, q * gw:(q + 1) * gw]


def mm_conv(a, w, conv_w, conv_b, lay, *, tn, out_dtype, gated=False, act=None,
            gw=None, col0=0, ncols=None, name="mm_conv"):
    tp, k = a.shape
    ksize = conv_w.shape[0]
    ncols = ncols if ncols is not None else w.shape[1]
    nj = ncols // tn
    c0 = col0 // tn
    assert col0 % tn == 0 and ncols % tn == 0
    assert gw is None or tn % gw == 0
    n_tiles = tp // TM
    hb = TM // HALO
    a_specs = [pl.BlockSpec((TM, k), lambda i, j: (i, 0)),
               pl.BlockSpec((HALO, k), lambda i, j: (jnp.maximum(i * hb - 1, 0), 0)),
               pl.BlockSpec((HALO, k), lambda i, j: (jnp.minimum((i + 1) * hb, tp // HALO - 1), 0))]
    offs = [c0, c0 + nj] if gated else [c0]
    w_specs = [pl.BlockSpec((k, tn), lambda i, j, o=o: (0, o + j)) for o in offs]
    cw_specs = [pl.BlockSpec((ksize, tn), lambda i, j, o=o: (0, o + j)) for o in offs]
    cb_specs = [pl.BlockSpec((1, tn), lambda i, j, o=o: (0, o + j)) for o in offs]
    nw = len(offs)
    if gw is not None:
        out_spec = pl.BlockSpec((tn // gw, TM, gw), lambda i, j: (j, i, 0))
        out_shape = jax.ShapeDtypeStruct((ncols // gw, tp, gw), out_dtype)
    else:
        out_spec = pl.BlockSpec((TM, tn), lambda i, j: (i, j))
        out_shape = jax.ShapeDtypeStruct((tp, ncols), out_dtype)
    cb2 = conv_b.reshape(1, -1)
    return pl.pallas_call(
        functools.partial(_mm_conv_kernel, ksize=ksize, gated=gated, act=act, tm=TM,
                          n_lat_tiles=lay.n_lat // TM, n_tiles=n_tiles, gw=gw),
        grid=(n_tiles, nj),
        in_specs=a_specs + w_specs + cw_specs + cb_specs,
        out_specs=out_spec,
        out_shape=out_shape,
        scratch_shapes=[pltpu.VMEM((TM + 2 * HALO, k), BF16)]
        + [pltpu.VMEM((TM + 2 * HALO, tn), F32) for _ in range(nw)],
        compiler_params=_cp(("parallel", "arbitrary")),
        name=name,
    )(a, a, a, *([w] * nw), *([conv_w] * nw), *([cb2] * nw))


def _chunk_index(k, lay, reverse, chunk):
    n_lc = lay.n_lat // chunk
    n_cc = lay.n_ctx // chunk
    if reverse:
        valid = n_lc + n_cc - 1 - k
    else:
        valid = jnp.where(k < n_cc, n_lc + k, k - n_cc)
    return jnp.where(k < n_lc + n_cc, valid, k)


N_SSD_IN = 11


def _ssd_scan_kernel(*refs, n_valid_chunks):
    fwd_in, rev_in = refs[:N_SSD_IN], refs[N_SSD_IN:2 * N_SSD_IN]
    yf_ref, yb_ref, hf_scr, hb_scr = refs[2 * N_SSD_IN:]
    k = pl.program_id(1)

    @pl.when(k == 0)
    def _():
        hf_scr[...] = jnp.zeros_like(hf_scr)
        hb_scr[...] = jnp.zeros_like(hb_scr)

    @pl.when(k >= n_valid_chunks)
    def _():
        yf_ref[...] = jnp.zeros_like(yf_ref)
        yb_ref[...] = jnp.zeros_like(yb_ref)

    @pl.when(k < n_valid_chunks)
    def _():
        pre_f = _ssd_chunk_setup(*fwd_in, reverse=False)
        pre_r = _ssd_chunk_setup(*rev_in, reverse=True)
        for p in range(SSD_HEADS_PER_GROUP // 2):
            _ssd_head_pair(p, pre_f, fwd_in[0], fwd_in[-1], yf_ref, hf_scr)
            _ssd_head_pair(p, pre_r, rev_in[0], rev_in[-1], yb_ref, hb_scr)


def _ssd_chunk_setup(xs_ref, b_ref, c_ref, bt_ref, dtc_ref, dtr_ref, biasc_ref, biasr_ref,
                     ac_ref, ar_ref, dsk_ref, *, reverse):
    L = SSD_CHUNK
    li = lax.broadcasted_iota(jnp.int32, (L, L), 0)
    si = lax.broadcasted_iota(jnp.int32, (L, L), 1)
    if reverse:
        mask = si >= li
        mask_t = li >= si
    else:
        mask = si <= li
        mask_t = li <= si
    tri = jnp.where(mask, 1.0, 0.0).astype(BF16)
    tri_t = jnp.where(mask_t, 1.0, 0.0).astype(BF16)

    dtc = _softplus(dtc_ref[...] + biasc_ref[...])
    adt_c = dtc * ac_ref[...]
    hi, mid, lo = _split3(adt_c)
    acs_c = _dot(tri, hi) + _dot(tri, mid) + _dot(tri, lo)
    dtr = _softplus(dtr_ref[...] + biasr_ref[...])
    adt_r = dtr * ar_ref[...]
    hi, mid, lo = _split3(adt_r)
    acs_r = _dot(hi, tri_t) + _dot(mid, tri_t) + _dot(lo, tri_t)
    tot_r = acs_r[:, 0:1] if reverse else acs_r[:, L - 1:L]
    wdt_r = jnp.exp(tot_r - acs_r) * dtr
    etot = jnp.exp(tot_r)
    ecol = jnp.exp(acs_c)

    cb16 = c_ref[...]
    bt16 = bt_ref[...]
    cbm = jnp.where(mask, _dot(cb16, bt16), 0.0)
    return dict(acs_c=acs_c, acs_r=acs_r, dtr=dtr, wdt_r=wdt_r, etot=etot, ecol=ecol, cbm=cbm,
                cf=cb16.astype(F32), btf=bt16.astype(F32))


def _ssd_head_pair(p, pre, xs_ref, dsk_ref, y_ref, h_scr):
    L = SSD_CHUNK
    lo_half = lax.broadcasted_iota(jnp.int32, (L, LANES), 1) < SSD_HEAD_DIM
    cols = slice(p * LANES, (p + 1) * LANES)
    xp = xs_ref[:, cols]
    xpb = xp.astype(BF16)
    hp = h_scr[p]
    hpb = hp.astype(BF16)
    ys, hs = [], []
    for e2 in range(2):
        e = 2 * p + e2
        col = pre["acs_c"][:, e:e + 1]
        row = pre["acs_r"][e:e + 1, :]
        ld = (pre["cbm"] * jnp.exp(jnp.minimum(col - row, 0.0)) * pre["dtr"][e:e + 1, :]).astype(BF16)
        lo_m = (pre["cf"] * pre["ecol"][:, e:e + 1]).astype(BF16)
        ls = (pre["btf"] * pre["wdt_r"][e:e + 1, :]).astype(BF16)
        ys.append(_dot(ld, xpb) + _dot(lo_m, hpb))
        hs.append(_dot(ls, xpb) + pre["etot"][e:e + 1, :] * hp)
    y_ref[:, cols] = jnp.where(lo_half, ys[0], ys[1]) + dsk_ref[:, cols] * xp
    h_scr[p] = jnp.where(lo_half, hs[0], hs[1])


def ssd_scan(xs, bc, bt, dt_cols, dt_rows, bias, a_neg, dskip, lay):
    g, tp, gw = xs.shape
    L = SSD_CHUNK
    n_chunks = tp // L
    n_valid = lay.n_valid // L
    hpg = SSD_HEADS_PER_GROUP

    def lane_pad(v):
        return jnp.pad(v.reshape(g, 1, hpg), ((0, 0), (0, 0), (0, LANES - hpg)))

    in_specs, args = [], []
    for d in range(2):
        cidx = functools.partial(_chunk_index, lay=lay, reverse=(d == 1), chunk=L)
        in_specs += [pl.BlockSpec((None, L, gw), lambda gi, k, c=cidx: (gi, c(k), 0)),
                     pl.BlockSpec((None, L, SSD_STATE), lambda gi, k, c=cidx: (gi, c(k), 0)),
                     pl.BlockSpec((None, L, SSD_STATE), lambda gi, k, c=cidx: (g + gi, c(k), 0)),
                     pl.BlockSpec((None, SSD_STATE, L), lambda gi, k, c=cidx: (gi, 0, c(k))),
                     pl.BlockSpec((None, L, LANES), lambda gi, k, c=cidx: (gi, c(k), 0)),
                     pl.BlockSpec((None, hpg, L), lambda gi, k, c=cidx: (gi, 0, c(k))),
                     pl.BlockSpec((None, 1, LANES), lambda gi, k: (gi, 0, 0)),
                     pl.BlockSpec((None, hpg, 1), lambda gi, k: (gi, 0, 0)),
                     pl.BlockSpec((None, 1, LANES), lambda gi, k: (gi, 0, 0)),
                     pl.BlockSpec((None, hpg, 1), lambda gi, k: (gi, 0, 0)),
                     pl.BlockSpec((None, 1, gw), lambda gi, k: (gi, 0, 0))]
        args += [xs, bc, bc, bt, dt_cols[d], dt_rows[d], lane_pad(bias[d]), bias[d].reshape(g, hpg, 1),
                 lane_pad(a_neg[d]), a_neg[d].reshape(g, hpg, 1),
                 jnp.repeat(dskip[d], SSD_HEAD_DIM).reshape(g, 1, gw)]
    assert len(args) == 2 * N_SSD_IN
    out_specs = [pl.BlockSpec((None, L, gw),
                              lambda gi, k, c=functools.partial(_chunk_index, lay=lay, reverse=(d == 1), chunk=L):
                              (gi, c(k), 0)) for d in range(2)]
    return pl.pallas_call(
        functools.partial(_ssd_scan_kernel, n_valid_chunks=n_valid),
        grid=(g, n_chunks),
        in_specs=in_specs,
        out_specs=out_specs,
        out_shape=[jax.ShapeDtypeStruct((g, tp, gw), F32)] * 2,
        scratch_shapes=[pltpu.VMEM((hpg // 2, SSD_STATE, LANES), F32)] * 2,
        compiler_params=_cp(("parallel", "arbitrary")),
        name="ssd_scan",
    )(*args)


def _ssd_gate_kernel(yf_ref, yb_ref, z_ref, nw_ref, a_ref):
    gw = yf_ref.shape[-1]
    for g in range(SSD_GROUPS):
        cols = slice(g * gw, (g + 1) * gw)
        gz = (yf_ref[g] + yb_ref[g]) * _silu(z_ref[:, cols])
        gn = gz * lax.rsqrt(jnp.mean(gz * gz, axis=-1, keepdims=True) + NORM_EPS)
        a_ref[:, cols] = (gn * nw_ref[:, cols]).astype(BF16)


def ssd_gate(yf, yb, z, norm_w):
    g, tp, gw = yf.shape
    d_inner = g * gw
    tm = TE
    return pl.pallas_call(
        _ssd_gate_kernel,
        grid=(tp // tm,),
        in_specs=[pl.BlockSpec((g, tm, gw), lambda i: (0, i, 0)),
                  pl.BlockSpec((g, tm, gw), lambda i: (0, i, 0)),
                  pl.BlockSpec((tm, d_inner), lambda i: (i, 0)),
                  pl.BlockSpec((1, d_inner), lambda i: (0, 0))],
        out_specs=pl.BlockSpec((tm, d_inner), lambda i: (i, 0)),
        out_shape=jax.ShapeDtypeStruct((tp, d_inner), BF16),
        compiler_params=_cp(("parallel",)),
        name="ssd_gate",
    )(yf, yb, z, norm_w.reshape(1, d_inner))


def ssd_mixer(h, lay, w_in, conv_w, conv_b, dt_bias, a_log, d_skip, norm_w):
    d_inner = SSD_GROUPS * SSD_HEADS_PER_GROUP * SSD_HEAD_DIM
    gn = SSD_GROUPS * SSD_STATE
    conv_dim = d_inner + 2 * gn
    n_heads = SSD_GROUPS * SSD_HEADS_PER_GROUP
    tp = h.shape[0]
    w16 = w_in.astype(BF16)
    w_rest = w16[:, d_inner:d_inner + conv_dim]
    z = mm(h, w16[:, :d_inner], tn=512, name="ssd_z")
    xs = mm_conv(h, w_rest, conv_w, conv_b, lay, tn=SSD_HEADS_PER_GROUP * SSD_HEAD_DIM, out_dtype=F32,
                 act="silu", gw=SSD_HEADS_PER_GROUP * SSD_HEAD_DIM, col0=0, ncols=d_inner, name="ssd_x")
    bc = mm_conv(h, w_rest, conv_w, conv_b, lay, tn=512, out_dtype=BF16, act="silu",
                 gw=SSD_STATE, col0=d_inner, ncols=2 * gn, name="ssd_bc")
    dt_pre = mm(h, w16[:, d_inner + conv_dim:], name="ssd_dt")
    bt = jnp.transpose(bc[:SSD_GROUPS], (0, 2, 1))
    a_neg = -jnp.exp(a_log.astype(F32))
    dt_cols, dt_rows = [], []
    for d in range(2):
        dt_d = dt_pre[:, d * n_heads:(d + 1) * n_heads].reshape(tp, SSD_GROUPS, SSD_HEADS_PER_GROUP)
        dt_cols.append(jnp.pad(jnp.transpose(dt_d, (1, 0, 2)),
                               ((0, 0), (0, 0), (0, LANES - SSD_HEADS_PER_GROUP))))
        dt_rows.append(jnp.transpose(dt_d, (1, 2, 0)))
    y_fwd, y_rev = ssd_scan(xs, bc, bt, dt_cols, dt_rows, dt_bias, a_neg, d_skip, lay)
    return ssd_gate(y_fwd, y_rev, z, norm_w)


LRU_TB = 128


def _lru_scan_kernel(u_ref, gw_ref, gb_ref, lam_ref, o_ref, carry, *, reverse, n_valid_chunks):
    k = pl.program_id(0)
    tb = LRU_TB
    bw = u_ref.shape[1] // LRU_BLOCKS

    @pl.when(k == 0)
    def _():
        carry[...] = jnp.zeros_like(carry)

    @pl.when(k >= n_valid_chunks)
    def _():
        o_ref[...] = jnp.zeros_like(o_ref)

    @pl.when(k < n_valid_chunks)
    def _():
        u = u_ref[...]
        ub = u.astype(BF16)

        def pre(z):
            parts = [_dot(ub[:, n * bw:(n + 1) * bw], gw_ref[z, n]) for n in range(LRU_BLOCKS)]
            return jnp.concatenate(parts, axis=1) + gb_ref[z]

        r = jax.nn.sigmoid(pre(0))
        gi = jax.nn.sigmoid(pre(1))
        log_a = (-LRU_C) * r * _softplus(-lam_ref[...])
        a = jnp.exp(log_a)
        th = jnp.tanh(log_a)
        b = jnp.sqrt(-2.0 * th / (1.0 - th)) * (gi * u)
        n_g = tb // SUBLANES
        w = a.shape[1]
        sub = lax.broadcasted_iota(jnp.int32, (tb, 1), 0) & (SUBLANES - 1)

        def roll_in_group(x, shift):
            return pltpu.roll(x.reshape(n_g, SUBLANES, w), shift, 1).reshape(tb, w)

        sh = 1
        while sh < SUBLANES:
            if reverse:
                keep = sub < SUBLANES - sh
                a_s = jnp.where(keep, roll_in_group(a, SUBLANES - sh), 1.0)
                b_s = jnp.where(keep, roll_in_group(b, SUBLANES - sh), 0.0)
            else:
                keep = sub >= sh
                a_s = jnp.where(keep, roll_in_group(a, sh), 1.0)
                b_s = jnp.where(keep, roll_in_group(b, sh), 0.0)
            b = a * b_s + b
            a = a * a_s
            sh *= 2
        c = carry[0:1, :]
        for gi in (range(n_g - 1, -1, -1) if reverse else range(n_g)):
            rows = slice(gi * SUBLANES, (gi + 1) * SUBLANES)
            hg = b[rows, :] + a[rows, :] * c
            o_ref[rows, :] = hg
            c = hg[0:1, :] if reverse else hg[SUBLANES - 1:SUBLANES, :]
        carry[0:1, :] = c


def lru_scan(u, gate_w, gate_b, lam, lay, reverse):
    tp, w = u.shape
    tb = LRU_TB
    bw = w // LRU_BLOCKS
    cidx = functools.partial(_chunk_index, lay=lay, reverse=reverse, chunk=tb)
    return pl.pallas_call(
        functools.partial(_lru_scan_kernel, reverse=reverse, n_valid_chunks=lay.n_valid // tb),
        grid=(tp // tb,),
        in_specs=[pl.BlockSpec((tb, w), lambda k: (cidx(k), 0)),
                  pl.BlockSpec((2, LRU_BLOCKS, bw, bw), lambda k: (0, 0, 0, 0)),
                  pl.BlockSpec((2, 1, w), lambda k: (0, 0, 0)),
                  pl.BlockSpec((1, w), lambda k: (0, 0))],
        out_specs=pl.BlockSpec((tb, w), lambda k: (cidx(k), 0)),
        out_shape=jax.ShapeDtypeStruct((tp, w), F32),
        scratch_shapes=[pltpu.VMEM((8, w), F32)],
        compiler_params=_cp(("arbitrary",)),
        name="lru_scan_rev" if reverse else "lru_scan_fwd",
    )(u, gate_w.astype(BF16), gate_b.reshape(2, 1, w), lam.reshape(1, w))


def _lru_gate_kernel(hf_ref, hb_ref, gp_ref, a_ref):
    a_ref[...] = ((hf_ref[...] + hb_ref[...]) * jax.nn.gelu(gp_ref[...], approximate=True)).astype(BF16)


def lru_gate(hf, hb, gate_pre):
    tp, w = hf.shape
    tm = TE
    spec = pl.BlockSpec((tm, w), lambda i: (i, 0))
    return pl.pallas_call(
        _lru_gate_kernel,
        grid=(tp // tm,),
        in_specs=[spec, spec, spec],
        out_specs=spec,
        out_shape=jax.ShapeDtypeStruct((tp, w), BF16),
        compiler_params=_cp(("parallel",)),
        name="lru_gate",
    )(hf, hb, gate_pre)


def lru_mixer(h, lay, w_in, conv_w, conv_b, gate_w, gate_b, lam):
    w = conv_w.shape[1]
    w16 = w_in.astype(BF16)
    gate_pre = mm(h, w16[:, :w], tn=512, name="lru_gate_proj")
    u = mm_conv(h, w16[:, w:], conv_w, conv_b, lay, tn=512, out_dtype=F32, name="lru_x")
    hs = [lru_scan(u, gate_w[d], gate_b[d], lam[d], lay, reverse=(d == 1)) for d in range(2)]
    return lru_gate(hs[0], hs[1], gate_pre)


def rope_tables(lay):
    n = lay.n_lat
    row = jnp.repeat(jnp.arange(n // GRID_W, dtype=F32), GRID_W)
    col = jnp.tile(jnp.arange(GRID_W, dtype=F32), n // GRID_W)
    n_freq = ROPE_DIM // 4
    inv = ROPE_THETA ** (-jnp.arange(n_freq, dtype=F32) / n_freq)
    ang = jnp.concatenate([row[:, None] * inv, col[:, None] * inv], axis=-1)
    cos, sin = jnp.cos(ang), jnp.sin(ang)
    zero = jnp.zeros_like(sin)
    c = jnp.tile(cos, (1, 4))
    s1 = jnp.tile(jnp.concatenate([-sin, zero], axis=1), (1, 2))
    s2 = jnp.tile(jnp.concatenate([zero, sin], axis=1), (1, 2))
    pad = lay.tp - n
    c = jnp.concatenate([c, jnp.ones((pad, LANES), F32)], axis=0)
    s1 = jnp.pad(s1, ((0, pad), (0, 0)))
    s2 = jnp.pad(s2, ((0, pad), (0, 0)))
    return c, s1, s2


MLA_QK_PAD = 256
MLA_HPS = 2


def _mla_q_kernel(a_ref, g_ref, w_ref, c_ref, s1_ref, s2_ref, qr_ref, qp_ref, nscr, *, scale):
    @pl.when(pl.program_id(1) == 0)
    def _():
        x = a_ref[...]
        nscr[...] = (x * lax.rsqrt(jnp.mean(x * x, axis=-1, keepdims=True) + NORM_EPS) * g_ref[...]).astype(BF16)

    acc = _dot(nscr[...], w_ref[...]) * scale
    c, s1, s2 = c_ref[...], s1_ref[...], s2_ref[...]
    for hh in range(MLA_HPS):
        a_h = acc[:, hh * MLA_QK_PAD:(hh + 1) * MLA_QK_PAD]
        qp_ref[hh] = a_h.astype(BF16)
        rot = _rope128(a_h[:, LANES:], c, s1, s2)
        qr_ref[hh] = jnp.concatenate([a_h[:, :LANES], rot], axis=1).astype(BF16)


def _mla_kv_kernel(a_ref, kr_ref, g_ref, w_ref, c_ref, s1_ref, s2_ref, k_ref, v_ref, nscr, krscr):
    @pl.when(pl.program_id(1) == 0)
    def _():
        x = a_ref[...]
        nscr[...] = (x * lax.rsqrt(jnp.mean(x * x, axis=-1, keepdims=True) + NORM_EPS) * g_ref[...]).astype(BF16)
        krscr[...] = _rope128(kr_ref[...], c_ref[...], s1_ref[...], s2_ref[...]).astype(BF16)

    acc = _dot(nscr[...], w_ref[...])
    ones = jnp.ones((acc.shape[0], LANES), BF16)
    for hh in range(MLA_HPS):
        a_h = acc[:, hh * (MLA_NOPE + MLA_V):(hh + 1) * (MLA_NOPE + MLA_V)]
        k_ref[hh] = jnp.concatenate([a_h[:, :MLA_NOPE].astype(BF16), krscr[...]], axis=1)
        v_ref[hh] = jnp.concatenate([a_h[:, MLA_NOPE:].astype(BF16), ones], axis=1)


MLA_TQ = 512
MLA_TK = 512
MLA_SUB = 256


def _mla_attn_kernel(qr_ref, qp_ref, k_ref, v_ref, o_ref, acc_ref, s_ref, *, n_lat, n_ctx):
    qi = pl.program_id(1)
    tq, tk, sub = MLA_TQ, MLA_TK, MLA_SUB
    nsub = tq // sub
    n_kt = n_lat // tk
    assert n_kt % 2 == 0
    rows = [slice(c * sub, (c + 1) * sub) for c in range(nsub)]

    def qk_lat(c, tile):
        start = pl.multiple_of(tile * tk, tk)
        return _dot_nt(qr_ref[rows[c], :], k_ref[pl.ds(start, tk), :])

    def consume(c, s, vv, m):
        m_new = jnp.maximum(m, jnp.max(s, axis=-1, keepdims=True))
        alpha = jnp.exp2(m - m_new)
        p = jnp.exp2(s - m_new).astype(BF16)
        acc_ref[rows[c], :] = alpha * acc_ref[rows[c], :] + _dot(p, vv)
        return m_new

    def v_tile(tile):
        return v_ref[pl.ds(pl.multiple_of(tile * tk, tk), tk), :]

    def half(ms, slot, tile):
        vv = v_tile(tile)
        out = []
        for c in range(nsub):
            s = s_ref[slot, rows[c], :]
            s_ref[1 - slot, rows[c], :] = qk_lat(c, tile + 1)
            out.append(consume(c, s, vv, ms[c]))
        return tuple(out)

    def finish():
        acc = acc_ref[...]
        o_ref[...] = (acc[:, :MLA_V] / acc[:, MLA_V:]).astype(o_ref.dtype)

    acc_ref[...] = jnp.zeros_like(acc_ref)
    m0 = tuple(jnp.full((sub, 1), -jnp.inf, F32) for _ in range(nsub))
    k_ctx = lambda: k_ref[pl.ds(n_lat, n_ctx), :]
    v_ctx = lambda: v_ref[pl.ds(n_lat, n_ctx), :]

    @pl.when(qi < n_lat // tq)
    def _():
        for c in range(nsub):
            s_ref[0, rows[c], :] = qk_lat(c, 0)

        def pair(pi, ms):
            ms = half(ms, 0, 2 * pi)
            return half(ms, 1, 2 * pi + 1)

        ms = lax.fori_loop(0, n_kt // 2 - 1, pair, m0)
        ms = half(ms, 0, n_kt - 2)
        vv = v_tile(n_kt - 1)
        for c in range(nsub):
            s = s_ref[1, rows[c], :]
            s_c = _dot_nt(qp_ref[rows[c], :], k_ctx())
            m = consume(c, s, vv, ms[c])
            consume(c, s_c, v_ctx(), m)
        finish()

    @pl.when(qi >= n_lat // tq)
    def _():
        for c in range(nsub):
            consume(c, _dot_nt(qp_ref[rows[c], :], k_ctx()), v_ctx(), m0[c])
        finish()


def mla_mixer(h, lay, tables, w_in, q_norm, kv_norm, w_q_up, w_kv_up):
    tp = h.shape[0]
    d = h.shape[1]
    c, s1, s2 = tables
    qk = MLA_NOPE + ROPE_DIM
    w_lat = jnp.concatenate([
        w_in[:, :MLA_Q_RANK], jnp.zeros((d, 256), F32),
        w_in[:, MLA_Q_RANK:MLA_Q_RANK + MLA_KV_RANK],
        w_in[:, MLA_Q_RANK + MLA_KV_RANK:], jnp.zeros((d, LANES - ROPE_DIM), F32)], axis=1).astype(BF16)
    lat = mm(h, w_lat, name="mla_lat")
    wq = jnp.pad(w_q_up.reshape(MLA_Q_RANK, MLA_HEADS, qk),
                 ((0, 0), (0, 0), (0, MLA_QK_PAD - qk))).reshape(MLA_Q_RANK, MLA_HEADS * MLA_QK_PAD).astype(BF16)
    n_t = tp // TM
    tab_spec = pl.BlockSpec((TM, LANES), lambda i, j: (i, 0))
    q_rot, q_plain = pl.pallas_call(
        functools.partial(_mla_q_kernel, scale=qk ** -0.5 * LOG2E),
        grid=(n_t, MLA_HEADS // MLA_HPS),
        in_specs=[pl.BlockSpec((TM, MLA_Q_RANK), lambda i, j: (i, 0)),
                  pl.BlockSpec((1, MLA_Q_RANK), lambda i, j: (0, 0)),
                  pl.BlockSpec((MLA_Q_RANK, MLA_HPS * MLA_QK_PAD), lambda i, j: (0, j)),
                  tab_spec, tab_spec, tab_spec],
        out_specs=[pl.BlockSpec((MLA_HPS, TM, MLA_QK_PAD), lambda i, j: (j, i, 0)),
                   pl.BlockSpec((MLA_HPS, TM, MLA_QK_PAD), lambda i, j: (j, i, 0))],
        out_shape=[jax.ShapeDtypeStruct((MLA_HEADS, tp, MLA_QK_PAD), BF16)] * 2,
        scratch_shapes=[pltpu.VMEM((TM, MLA_Q_RANK), BF16)],
        compiler_params=_cp(("parallel", "arbitrary")),
        name="mla_q",
    )(lat, q_norm.reshape(1, -1), wq, c, s1, s2)
    kv_col = (MLA_Q_RANK + 256) // MLA_KV_RANK
    kr_col = (MLA_Q_RANK + 256 + MLA_KV_RANK) // LANES
    k_all, v_all = pl.pallas_call(
        _mla_kv_kernel,
        grid=(n_t, MLA_HEADS // MLA_HPS),
        in_specs=[pl.BlockSpec((TM, MLA_KV_RANK), lambda i, j: (i, kv_col)),
                  pl.BlockSpec((TM, LANES), lambda i, j: (i, kr_col)),
                  pl.BlockSpec((1, MLA_KV_RANK), lambda i, j: (0, 0)),
                  pl.BlockSpec((MLA_KV_RANK, MLA_HPS * (MLA_NOPE + MLA_V)), lambda i, j: (0, j)),
                  tab_spec, tab_spec, tab_spec],
        out_specs=[pl.BlockSpec((MLA_HPS, TM, MLA_QK_PAD), lambda i, j: (j, i, 0)),
                   pl.BlockSpec((MLA_HPS, TM, MLA_V + LANES), lambda i, j: (j, i, 0))],
        out_shape=[jax.ShapeDtypeStruct((MLA_HEADS, tp, MLA_QK_PAD), BF16),
                   jax.ShapeDtypeStruct((MLA_HEADS, tp, MLA_V + LANES), BF16)],
        scratch_shapes=[pltpu.VMEM((TM, MLA_KV_RANK), BF16), pltpu.VMEM((TM, LANES), BF16)],
        compiler_params=_cp(("parallel", "arbitrary")),
        name="mla_kv",
    )(lat, lat, kv_norm.reshape(1, -1), w_kv_up.astype(BF16), c, s1, s2)
    tq = MLA_TQ
    return pl.pallas_call(
        functools.partial(_mla_attn_kernel, n_lat=lay.n_lat, n_ctx=lay.n_ctx),
        grid=(MLA_HEADS, tp // tq),
        in_specs=[pl.BlockSpec((None, tq, MLA_QK_PAD), lambda hh, i: (hh, i, 0)),
                  pl.BlockSpec((None, tq, MLA_QK_PAD), lambda hh, i: (hh, i, 0)),
                  pl.BlockSpec((None, tp, MLA_QK_PAD), lambda hh, i: (hh, 0, 0)),
                  pl.BlockSpec((None, tp, MLA_V + LANES), lambda hh, i: (hh, 0, 0))],
        out_specs=pl.BlockSpec((tq, MLA_V), lambda hh, i: (i, hh)),
        out_shape=jax.ShapeDtypeStruct((tp, MLA_HEADS * MLA_V), BF16),
        scratch_shapes=[pltpu.VMEM((tq, MLA_V + LANES), F32), pltpu.VMEM((2, tq, MLA_TK), F32)],
        compiler_params=_cp(("parallel", "parallel")),
        name="mla_attn",
    )(q_rot, q_plain, k_all, v_all)


def _mm_bias_rope_kernel(*refs, scale, rope, emit_plain):
    a_ref, w_ref, b_ref = refs[:3]
    if rope:
        c_ref, s1_ref, s2_ref = refs[3:6]
        outs = refs[6:]
    else:
        outs = refs[3:]
    acc = (_dot(a_ref[...], w_ref[...]) + b_ref[...]) * scale
    if not rope:
        outs[0][...] = acc.astype(outs[0].dtype)
        return
    c, s1, s2 = c_ref[...], s1_ref[...], s2_ref[...]
    tn = acc.shape[1]
    for s in range(tn // LANES):
        cols = slice(s * LANES, (s + 1) * LANES)
        outs[0][:, cols] = _rope128(acc[:, cols], c, s1, s2).astype(outs[0].dtype)
    if emit_plain:
        outs[1][...] = acc.astype(outs[1].dtype)


def mm_bias_rope(a, w, b, tables, *, scale=1.0, rope=True, emit_plain=False, name="mm_bias_rope"):
    m, k = a.shape
    n = w.shape[1]
    tn = 512
    in_specs = [pl.BlockSpec((TM, k), lambda i, j: (i, 0)),
                pl.BlockSpec((k, tn), lambda i, j: (0, j)),
                pl.BlockSpec((1, tn), lambda i, j: (0, j))]
    args = [a, w, b.reshape(1, n)]
    if rope:
        in_specs += [pl.BlockSpec((TM, LANES), lambda i, j: (i, 0))] * 3
        args += list(tables)
    n_out = 2 if (rope and emit_plain) else 1
    out_spec = pl.BlockSpec((TM, tn), lambda i, j: (i, j))
    res = pl.pallas_call(
        functools.partial(_mm_bias_rope_kernel, scale=scale, rope=rope, emit_plain=emit_plain),
        grid=(m // TM, n // tn),
        in_specs=in_specs,
        out_specs=[out_spec] * n_out,
        out_shape=[jax.ShapeDtypeStruct((m, n), BF16)] * n_out,
        compiler_params=_cp(("parallel", "parallel")),
        name=name,
    )(*args)
    return res if n_out == 2 else res[0]


SWA_BPS = 2


def _swa_attn_kernel(qr_ref, qp_ref, kp_ref, kc_ref, kn_ref, vp_ref, vc_ref, vn_ref,
                     kctx_ref, vctx_ref, sink_ref, o_ref, *, n_lat):
    b = pl.program_id(1)
    blk = ATTN_BLOCK
    nb = n_lat // blk

    @pl.when(b * SWA_BPS >= nb)
    def _():
        o_ref[...] = jnp.zeros_like(o_ref)

    @pl.when(b * SWA_BPS < nb)
    def _():
        lane = lax.broadcasted_iota(jnp.int32, (blk, LANES), 1)
        r = lax.broadcasted_iota(jnp.int32, (blk, blk), 0)
        c = lax.broadcasted_iota(jnp.int32, (blk, blk), 1)
        lo_half = lane < SWA_HEAD_DIM
        kwin = jnp.concatenate([kp_ref[...], kc_ref[...], kn_ref[...]], axis=0)
        vwin = jnp.concatenate([vp_ref[...], vc_ref[...], vn_ref[...]], axis=0)
        ones_w = jnp.ones(vwin.shape, BF16)
        vwe = jnp.concatenate([vwin, ones_w], axis=1)
        vce = jnp.concatenate([vctx_ref[...], ones_w[:vctx_ref.shape[0]]], axis=1)
        kctx = kctx_ref[...]
        n_c = kctx.shape[0]
        scores = []
        for t in range(SWA_BPS):
            gb = b * SWA_BPS + t
            rows = slice(t * blk, (t + 1) * blk)
            win = slice(t * blk, (t + 3) * blk)
            bias_p = jnp.where((c >= r) & (gb > 0), 0.0, -jnp.inf)
            bias_n = jnp.where((c <= r) & (gb < nb - 1), 0.0, -jnp.inf)
            for g in range(SWA_GROUP):
                cols = slice((g // 2) * LANES, (g // 2 + 1) * LANES)
                keep = lo_half if g % 2 == 0 else jnp.logical_not(lo_half)
                qr = qr_ref[rows, cols]
                qp = qp_ref[rows, cols]
                qr = jnp.where(keep, qr, jnp.zeros_like(qr))
                qp = jnp.where(keep, qp, jnp.zeros_like(qp))
                s_c = _dot_nt(qp, kctx)
                s_w = _dot_nt(qr, kwin[win])
                scores.append(jnp.concatenate([s_c, s_w[:, :blk] + bias_p, s_w[:, blk:2 * blk],
                                               s_w[:, 2 * blk:] + bias_n], axis=1))
        for t in range(SWA_BPS):
            rows = slice(t * blk, (t + 1) * blk)
            win = slice(t * blk, (t + 3) * blk)
            outs = []
            for g in range(SWA_GROUP):
                s = scores[t * SWA_GROUP + g]
                sk = sink_ref[g:g + 1, 0:1] * LOG2E
                m = jnp.maximum(jnp.max(s, axis=-1, keepdims=True), sk)
                p = jnp.exp2(s - m).astype(BF16)
                oe = _dot(p[:, :n_c], vce) + _dot(p[:, n_c:], vwe[win])
                outs.append(oe[:, :LANES] / (oe[:, LANES:] + jnp.exp2(sk - m)))
            o_ref[rows, :] = jnp.concatenate(
                [jnp.where(lo_half, outs[0], outs[1]), jnp.where(lo_half, outs[2], outs[3])],
                axis=1).astype(o_ref.dtype)


def swa_mixer(h, lay, tables, w_qkv, b_qkv, sink):
    tp = h.shape[0]
    d = h.shape[1]
    hd = SWA_HEAD_DIM
    qd = SWA_Q_HEADS * hd
    kd = SWA_KV_HEADS * hd

    def dup_heads(wm):
        lead = wm.shape[:-1]
        w3 = wm.reshape(lead + (SWA_KV_HEADS, 1, hd))
        return jnp.broadcast_to(w3, lead + (SWA_KV_HEADS, 2, hd)).reshape(lead + (2 * kd,))

    w16 = w_qkv.astype(BF16)
    q_rot, q_plain = mm_bias_rope(h, w16[:, :qd], b_qkv[:qd], tables, scale=hd ** -0.5 * LOG2E,
                                  emit_plain=True, name="swa_q")
    k_dup = mm_bias_rope(h, dup_heads(w16[:, qd:qd + kd]), dup_heads(b_qkv[qd:qd + kd]), tables, name="swa_k")
    v_dup = mm_bias_rope(h, dup_heads(w16[:, qd + kd:]), dup_heads(b_qkv[qd + kd:]), tables,
                         rope=False, name="swa_v")
    blk = ATTN_BLOCK
    nb = lay.n_lat // blk
    ctx_blk = lay.n_lat // lay.n_ctx
    sink_t = jnp.broadcast_to(jnp.pad(sink.astype(F32).reshape(SWA_KV_HEADS, SWA_GROUP),
                                      ((0, 0), (0, 8 - SWA_GROUP)))[:, :, None], (SWA_KV_HEADS, 8, LANES))
    qw = SWA_GROUP * hd
    bps = SWA_BPS
    assert nb % bps == 0 and tp % (blk * bps) == 0
    q_spec = pl.BlockSpec((blk * bps, qw), lambda hh, b: (b, hh))
    prev_spec = pl.BlockSpec((blk, LANES), lambda hh, b: (jnp.clip(b * bps - 1, 0, nb - 1), hh))
    cur_spec = pl.BlockSpec((blk * bps, LANES), lambda hh, b: (jnp.minimum(b, nb // bps - 1), hh))
    next_spec = pl.BlockSpec((blk, LANES), lambda hh, b: (jnp.clip(b * bps + bps, 0, nb - 1), hh))
    ctx_spec = pl.BlockSpec((lay.n_ctx, LANES), lambda hh, b: (ctx_blk, hh))
    return pl.pallas_call(
        functools.partial(_swa_attn_kernel, n_lat=lay.n_lat),
        grid=(SWA_KV_HEADS, tp // (blk * bps)),
        in_specs=[q_spec, q_spec, prev_spec, cur_spec, next_spec,
                  prev_spec, cur_spec, next_spec, ctx_spec, ctx_spec,
                  pl.BlockSpec((None, 8, LANES), lambda hh, b: (hh, 0, 0))],
        out_specs=pl.BlockSpec((blk * bps, qw), lambda hh, b: (b, hh)),
        out_shape=jax.ShapeDtypeStruct((tp, qd), BF16),
        compiler_params=_cp(("parallel", "parallel")),
        name="swa_attn",
    )(q_rot, q_plain, k_dup, k_dup, k_dup, v_dup, v_dup, v_dup, k_dup, v_dup, sink_t)


def conv_ffn_hidden(h, lay, w_up, conv_w, conv_b):
    hidden = w_up.shape[1] // 2
    return mm_conv(h, w_up.astype(BF16), conv_w, conv_b, lay, tn=512, out_dtype=BF16, gated=True,
                   ncols=hidden, name="ffn_up")


def kernel(x, c, ctx, c_ctx, ada_w, ada_b, norm_mix, norm_ffn, ffn_up, ffn_conv_w, ffn_conv_b, ffn_down, final_norm, ssd_in, ssd_conv_w, ssd_conv_b, ssd_dt_bias, ssd_a_log, ssd_d, ssd_norm, ssd_out, lru_in, lru_conv_w, lru_conv_b, lru_gate_w, lru_gate_b, lru_lambda, lru_out, mla_in, mla_q_norm, mla_kv_norm, mla_q_up, mla_kv_up, mla_out, swa_qkv, swa_qkv_b, swa_sink, swa_out):
    bsz, n_lat, d = x.shape
    n_ctx = ctx.shape[1]
    depth = ada_w.shape[0]
    assert bsz == 1 and depth == 4, "one sample, four layers (one of each mixer)"
    lay = make_layout(n_lat, n_ctx)
    xs = jnp.concatenate([x[0], ctx[0], jnp.zeros((lay.tp - lay.n_valid, d), x.dtype)], axis=0)
    cv = jnp.concatenate([c, c_ctx[None, :], jnp.zeros((6, d), F32)], axis=0)
    mods = adaln(cv, ada_w, ada_b)
    tables = rope_tables(lay)
    for i in range(depth):
        mod = jnp.pad(mods[i, :2].reshape(2, ADA_CHUNKS, d), ((0, 0), (0, 8 - ADA_CHUNKS), (0, 0)))
        h = norm_mod(xs, norm_mix[i], mod, 0, 1, lay)
        if i == 0:
            a = ssd_mixer(h, lay, ssd_in[0], ssd_conv_w[0], ssd_conv_b[0], ssd_dt_bias[0], ssd_a_log[0],
                          ssd_d[0], ssd_norm[0])
            w_o = ssd_out[0]
        elif i == 1:
            a = lru_mixer(h, lay, lru_in[0], lru_conv_w[0], lru_conv_b[0], lru_gate_w[0], lru_gate_b[0],
                          lru_lambda[0])
            w_o = lru_out[0]
        elif i == 2:
            a = mla_mixer(h, lay, tables, mla_in[0], mla_q_norm[0], mla_kv_norm[0], mla_q_up[0], mla_kv_up[0])
            w_o = mla_out[0]
        else:
            a = swa_mixer(h, lay, tables, swa_qkv[0], swa_qkv_b[0], swa_sink[0])
            w_o = swa_out[0]
        xs = mm_resid(a, w_o.astype(BF16), xs, mod, 2, lay, name="mix_out")
        h = norm_mod(xs, norm_ffn[i], mod, 3, 4, lay)
        hid = conv_ffn_hidden(h, lay, ffn_up[i], ffn_conv_w[i], ffn_conv_b[i])
        xs = mm_resid(hid, ffn_down[i].astype(BF16), xs, mod, 5, lay, name="ffn_down")
    return rmsnorm_rows(xs, final_norm, n_lat)[None]
```

```python
import functools
import math
from typing import NamedTuple

import jax
import jax.numpy as jnp
from jax import lax
from jax.experimental import pallas as pl
from jax.experimental.pallas import tpu as pltpu

F32 = jnp.float32
BF16 = jnp.bfloat16

GRID_W = 64
NORM_EPS = 1e-6
ADA_CHUNKS = 6
ROPE_THETA = 10000.0
ROPE_DIM = 64
SSD_HEAD_DIM = 64
SSD_GROUPS = 8
SSD_HEADS_PER_GROUP = 8
SSD_STATE = 128
SSD_CHUNK = 128
LRU_BLOCKS = 8
LRU_C = 8.0
MLA_HEADS = 16
MLA_Q_RANK = 768
MLA_KV_RANK = 512
MLA_NOPE = 128
MLA_V = 128
SWA_Q_HEADS = 32
SWA_KV_HEADS = 8
SWA_GROUP = 4
SWA_HEAD_DIM = 64
SWA_WINDOW = 128
ATTN_BLOCK = 128

LANES = 128
SUBLANES = 8
MXU_COLS = 256
BF16_ROWS = 16
TM = 512
TE = 256
RESID_ROW_TILES = 8
VMEM_LIMIT = 56 * 2 ** 20
LOG2E = math.log2(math.e)


class Layout(NamedTuple):
    n_lat: int
    n_ctx: int
    tp: int

    @property
    def n_valid(self):
        return self.n_lat + self.n_ctx


def make_layout(n_lat, n_ctx):
    assert n_lat % TM == 0 and n_ctx % SSD_CHUNK == 0
    ctx_pad = -(-n_ctx // TM) * TM
    return Layout(n_lat, n_ctx, n_lat + ctx_pad)


def _cp(sem):
    return pltpu.CompilerParams(dimension_semantics=sem, vmem_limit_bytes=VMEM_LIMIT)


def _silu(x):
    return x * jax.nn.sigmoid(x)


def _softplus(x):
    return jnp.maximum(x, 0.0) + jnp.log1p(jnp.exp(-jnp.abs(x)))


def _dot(a, b):
    return jnp.dot(a, b, preferred_element_type=F32)


def _dot_nt(a, b):
    return lax.dot_general(a, b, (((1,), (1,)), ((), ())), preferred_element_type=F32)


def _split3(x):
    hi = x.astype(BF16)
    r = x - hi.astype(F32)
    mid = r.astype(BF16)
    lo = (r - mid.astype(F32)).astype(BF16)
    return hi, mid, lo


def _rope128(v, c, s1, s2):
    return v * c + pltpu.roll(v, 96, 1) * s1 + pltpu.roll(v, 32, 1) * s2


def _adaln_kernel(cv_ref, w_ref, b_ref, o_ref):
    s = _silu(cv_ref[...]).astype(BF16)
    o_ref[...] = _dot(s, w_ref[...].astype(BF16)) + b_ref[...]


def adaln(cv, ada_w, ada_b):
    depth, d, n = ada_w.shape
    tn = 1024
    return pl.pallas_call(
        _adaln_kernel,
        grid=(depth, n // tn),
        in_specs=[pl.BlockSpec((8, d), lambda l, j: (0, 0)),
                  pl.BlockSpec((None, d, tn), lambda l, j: (l, 0, j)),
                  pl.BlockSpec((None, 1, tn), lambda l, j: (l, 0, j))],
        out_specs=pl.BlockSpec((None, 8, tn), lambda l, j: (l, 0, j)),
        out_shape=jax.ShapeDtypeStruct((depth, 8, n), F32),
        compiler_params=_cp(("parallel", "parallel")),
        name="adaln",
    )(cv, ada_w, ada_b.reshape(depth, 1, n))


def _norm_mod_kernel(x_ref, g_ref, mod_ref, h_ref, *, shift_idx, scale_idx, n_valid, tm):
    x = x_ref[...]
    y = x * lax.rsqrt(jnp.mean(x * x, axis=-1, keepdims=True) + NORM_EPS) * g_ref[...]
    h = y * (1.0 + mod_ref[scale_idx:scale_idx + 1, :]) + mod_ref[shift_idx:shift_idx + 1, :]
    row = pl.program_id(0) * tm + lax.broadcasted_iota(jnp.int32, (tm, 1), 0)
    h_ref[...] = jnp.where(row < n_valid, h, 0.0).astype(BF16)


def norm_mod(x, g, mod, shift_idx, scale_idx, lay):
    tp, d = x.shape
    tm = TM
    return pl.pallas_call(
        functools.partial(_norm_mod_kernel, shift_idx=shift_idx, scale_idx=scale_idx,
                          n_valid=lay.n_valid, tm=tm),
        grid=(tp // tm,),
        in_specs=[pl.BlockSpec((tm, d), lambda i: (i, 0)),
                  pl.BlockSpec((1, d), lambda i: (0, 0)),
                  pl.BlockSpec((None, 8, d), lambda i: ((i * tm >= lay.n_lat).astype(jnp.int32), 0, 0))],
        out_specs=pl.BlockSpec((tm, d), lambda i: (i, 0)),
        out_shape=jax.ShapeDtypeStruct((tp, d), BF16),
        compiler_params=_cp(("parallel",)),
        name="norm_mod",
    )(x, g.reshape(1, d), mod)


def _rmsnorm_kernel(x_ref, g_ref, o_ref):
    x = x_ref[...]
    o_ref[...] = x * lax.rsqrt(jnp.mean(x * x, axis=-1, keepdims=True) + NORM_EPS) * g_ref[...]


def rmsnorm_rows(x, g, rows):
    d = x.shape[1]
    tm = TM
    return pl.pallas_call(
        _rmsnorm_kernel,
        grid=(rows // tm,),
        in_specs=[pl.BlockSpec((tm, d), lambda i: (i, 0)), pl.BlockSpec((1, d), lambda i: (0, 0))],
        out_specs=pl.BlockSpec((tm, d), lambda i: (i, 0)),
        out_shape=jax.ShapeDtypeStruct((rows, d), F32),
        compiler_params=_cp(("parallel",)),
        name="final_norm",
    )(x, g.reshape(1, d))


def _mm_kernel(a_ref, w_ref, o_ref):
    o_ref[...] = _dot(a_ref[...], w_ref[...]).astype(o_ref.dtype)


def mm(a, w, out_dtype=F32, tn=None, name="mm"):
    m, k = a.shape
    n = w.shape[1]
    tn = n if tn is None else tn
    tm = m // RESID_ROW_TILES
    assert m % RESID_ROW_TILES == 0 and tm % BF16_ROWS == 0
    return pl.pallas_call(
        _mm_kernel,
        grid=(RESID_ROW_TILES, n // tn),
        in_specs=[pl.BlockSpec((tm, k), lambda i, j: (i, 0)),
                  pl.BlockSpec((k, tn), lambda i, j: (0, j))],
        out_specs=pl.BlockSpec((tm, tn), lambda i, j: (i, j)),
        out_shape=jax.ShapeDtypeStruct((m, n), out_dtype),
        compiler_params=_cp(("parallel", "parallel")),
        name=name,
    )(a, w)


def _mm_resid_kernel(a_ref, w_ref, x_ref, mod_ref, o_ref, *, gate_idx, n_lat, tm):
    acc = _dot(a_ref[...], w_ref[...])
    row = pl.program_id(0) * tm + lax.broadcasted_iota(jnp.int32, (tm, 1), 0)
    gate = jnp.where(row < n_lat, mod_ref[0, gate_idx:gate_idx + 1, :], mod_ref[1, gate_idx:gate_idx + 1, :])
    o_ref[...] = x_ref[...] + gate * acc


def mm_resid(a, w, x, mod, gate_idx, lay, name="mm_resid"):
    m, k = a.shape
    n = w.shape[1]
    tn = 512
    tm = m // RESID_ROW_TILES
    assert m % RESID_ROW_TILES == 0 and tm % BF16_ROWS == 0
    return pl.pallas_call(
        functools.partial(_mm_resid_kernel, gate_idx=gate_idx, n_lat=lay.n_lat, tm=tm),
        grid=(RESID_ROW_TILES, n // tn),
        in_specs=[pl.BlockSpec((tm, k), lambda i, j: (i, 0)),
                  pl.BlockSpec((k, tn), lambda i, j: (0, j)),
                  pl.BlockSpec((tm, tn), lambda i, j: (i, j)),
                  pl.BlockSpec((2, 8, tn), lambda i, j: (0, 0, j))],
        out_specs=pl.BlockSpec((tm, tn), lambda i, j: (i, j)),
        out_shape=jax.ShapeDtypeStruct((m, n), F32),
        compiler_params=_cp(("parallel", "parallel")),
        name=name,
    )(a, w, x, mod)


HALO = BF16_ROWS


def _mm_conv_kernel(*refs, ksize, gated, act, tm, n_lat_tiles, n_tiles, gw):
    nw = 2 if gated else 1
    a_ref, ap_ref, an_ref = refs[:3]
    w_refs = refs[3:3 + nw]
    cw_refs = refs[3 + nw:3 + 2 * nw]
    cb_refs = refs[3 + 2 * nw:3 + 3 * nw]
    o_ref = refs[3 + 3 * nw]
    hbuf = refs[4 + 3 * nw]
    u_scrs = refs[5 + 3 * nw:]
    i = pl.program_id(0)
    tn = w_refs[0].shape[1]

    @pl.when(pl.program_id(1) == 0)
    def _():
        first = (i == 0) | (i == n_lat_tiles)
        last = (i == n_lat_tiles - 1) | (i == n_tiles - 1)
        prev = ap_ref[...]
        nxt = an_ref[...]
        hbuf[0:HALO, :] = jnp.where(first, jnp.zeros_like(prev), prev)
        hbuf[HALO:HALO + tm, :] = a_ref[...]
        hbuf[HALO + tm:, :] = jnp.where(last, jnp.zeros_like(nxt), nxt)

    left = (ksize - 1) // 2

    def conv(wi):
        u_scrs[wi][...] = _dot(hbuf[...], w_refs[wi][...])
        acc = cb_refs[wi][...] + cw_refs[wi][0:1, :] * u_scrs[wi][pl.ds(HALO - left, tm), :]
        for kk in range(1, ksize):
            acc = acc + cw_refs[wi][kk:kk + 1, :] * u_scrs[wi][pl.ds(HALO - left + kk, tm), :]
        return acc

    if gated:
        out = _silu(conv(0)) * conv(1)
    else:
        out = conv(0)
        if act == "silu":
            out = _silu(out)
    out = out.astype(o_ref.dtype)
    if gw is None:
        o_ref[...] = out
    else:
        for q in range(tn // gw):
            o_ref[q] = out[:, q * gw:(q + 1) * gw]


def mm_conv(a, w, conv_w, conv_b, lay, *, tn, out_dtype, gated=False, act=None,
            gw=None, col0=0, ncols=None, name="mm_conv"):
    tp, k = a.shape
    ksize = conv_w.shape[0]
    ncols = ncols if ncols is not None else w.shape[1]
    nj = ncols // tn
    c0 = col0 // tn
    assert col0 % tn == 0 and ncols % tn == 0
    assert gw is None or tn % gw == 0
    n_tiles = tp // TM
    hb = TM // HALO
    a_specs = [pl.BlockSpec((TM, k), lambda i, j: (i, 0)),
               pl.BlockSpec((HALO, k), lambda i, j: (jnp.maximum(i * hb - 1, 0), 0)),
               pl.BlockSpec((HALO, k), lambda i, j: (jnp.minimum((i + 1) * hb, tp // HALO - 1), 0))]
    offs = [c0, c0 + nj] if gated else [c0]
    w_specs = [pl.BlockSpec((k, tn), lambda i, j, o=o: (0, o + j)) for o in offs]
    cw_specs = [pl.BlockSpec((ksize, tn), lambda i, j, o=o: (0, o + j)) for o in offs]
    cb_specs = [pl.BlockSpec((1, tn), lambda i, j, o=o: (0, o + j)) for o in offs]
    nw = len(offs)
    if gw is not None:
        out_spec = pl.BlockSpec((tn // gw, TM, gw), lambda i, j: (j, i, 0))
        out_shape = jax.ShapeDtypeStruct((ncols // gw, tp, gw), out_dtype)
    else:
        out_spec = pl.BlockSpec((TM, tn), lambda i, j: (i, j))
        out_shape = jax.ShapeDtypeStruct((tp, ncols), out_dtype)
    cb2 = conv_b.reshape(1, -1)
    return pl.pallas_call(
        functools.partial(_mm_conv_kernel, ksize=ksize, gated=gated, act=act, tm=TM,
                          n_lat_tiles=lay.n_lat // TM, n_tiles=n_tiles, gw=gw),
        grid=(n_tiles, nj),
        in_specs=a_specs + w_specs + cw_specs + cb_specs,
        out_specs=out_spec,
        out_shape=out_shape,
        scratch_shapes=[pltpu.VMEM((TM + 2 * HALO, k), BF16)]
        + [pltpu.VMEM((TM + 2 * HALO, tn), F32) for _ in range(nw)],
        compiler_params=_cp(("parallel", "arbitrary")),
        name=name,
    )(a, a, a, *([w] * nw), *([conv_w] * nw), *([cb2] * nw))


def _chunk_index(k, lay, reverse, chunk):
    n_lc = lay.n_lat // chunk
    n_cc = lay.n_ctx // chunk
    if reverse:
        valid = n_lc + n_cc - 1 - k
    else:
        valid = jnp.where(k < n_cc, n_lc + k, k - n_cc)
    return jnp.where(k < n_lc + n_cc, valid, k)


N_SSD_IN = 11
SSD_GPS = 2


def _ssd_scan_kernel(*refs, n_valid_chunks):
    fwd_in, rev_in = refs[:N_SSD_IN], refs[N_SSD_IN:2 * N_SSD_IN]
    yf_ref, yb_ref, hf_scr, hb_scr = refs[2 * N_SSD_IN:]
    k = pl.program_id(1)

    @pl.when(k == 0)
    def _():
        hf_scr[...] = jnp.zeros_like(hf_scr)
        hb_scr[...] = jnp.zeros_like(hb_scr)

    @pl.when(k >= n_valid_chunks)
    def _():
        yf_ref[...] = jnp.zeros_like(yf_ref)
        yb_ref[...] = jnp.zeros_like(yb_ref)

    @pl.when(k < n_valid_chunks)
    def _():
        chains = []
        for gi in range(SSD_GPS):
            for ins, y_ref, h_scr, rev in ((fwd_in, yf_ref, hf_scr, False), (rev_in, yb_ref, hb_scr, True)):
                views = [r.at[gi] for r in ins]
                chains.append((_ssd_chunk_setup(*views, reverse=rev), views[0], views[-1],
                               y_ref.at[gi], h_scr.at[gi]))
        for p in range(SSD_HEADS_PER_GROUP // 2):
            for pre, xs_v, dsk_v, y_v, h_v in chains:
                _ssd_head_pair(p, pre, xs_v, dsk_v, y_v, h_v)


def _ssd_chunk_setup(xs_ref, b_ref, c_ref, bt_ref, dtc_ref, dtr_ref, biasc_ref, biasr_ref,
                     ac_ref, ar_ref, dsk_ref, *, reverse):
    L = SSD_CHUNK
    li = lax.broadcasted_iota(jnp.int32, (L, L), 0)
    si = lax.broadcasted_iota(jnp.int32, (L, L), 1)
    if reverse:
        mask = si >= li
        mask_t = li >= si
    else:
        mask = si <= li
        mask_t = li <= si
    tri = jnp.where(mask, 1.0, 0.0).astype(BF16)
    tri_t = jnp.where(mask_t, 1.0, 0.0).astype(BF16)

    dtc = _softplus(dtc_ref[...] + biasc_ref[...])
    adt_c = dtc * ac_ref[...]
    hi, mid, lo = _split3(adt_c)
    acs_c = _dot(tri, hi) + _dot(tri, mid) + _dot(tri, lo)
    dtr = _softplus(dtr_ref[...] + biasr_ref[...])
    adt_r = dtr * ar_ref[...]
    hi, mid, lo = _split3(adt_r)
    acs_r = _dot(hi, tri_t) + _dot(mid, tri_t) + _dot(lo, tri_t)
    tot_r = acs_r[:, 0:1] if reverse else acs_r[:, L - 1:L]
    wdt_r = jnp.exp(tot_r - acs_r) * dtr
    etot = jnp.exp(tot_r)
    ecol = jnp.exp(acs_c)

    cb16 = c_ref[...]
    bt16 = bt_ref[...]
    cbm = jnp.where(mask, _dot(cb16, bt16), 0.0)
    return dict(acs_c=acs_c, acs_r=acs_r, dtr=dtr, wdt_r=wdt_r, etot=etot, ecol=ecol, cbm=cbm,
                cf=cb16.astype(F32), btf=bt16.astype(F32))


def _ssd_head_pair(p, pre, xs_ref, dsk_ref, y_ref, h_scr):
    L = SSD_CHUNK
    lo_half = lax.broadcasted_iota(jnp.int32, (L, LANES), 1) < SSD_HEAD_DIM
    cols = slice(p * LANES, (p + 1) * LANES)
    xp = xs_ref[:, cols]
    xpb = xp.astype(BF16)
    hp = h_scr[p]
    hpb = hp.astype(BF16)
    ys, hs = [], []
    for e2 in range(2):
        e = 2 * p + e2
        col = pre["acs_c"][:, e:e + 1]
        row = pre["acs_r"][e:e + 1, :]
        ld = (pre["cbm"] * jnp.exp(jnp.minimum(col - row, 0.0)) * pre["dtr"][e:e + 1, :]).astype(BF16)
        lo_m = (pre["cf"] * pre["ecol"][:, e:e + 1]).astype(BF16)
        ls = (pre["btf"] * pre["wdt_r"][e:e + 1, :]).astype(BF16)
        ys.append(_dot(ld, xpb) + _dot(lo_m, hpb))
        hs.append(_dot(ls, xpb) + pre["etot"][e:e + 1, :] * hp)
    y_ref[:, cols] = jnp.where(lo_half, ys[0], ys[1]) + dsk_ref[:, cols] * xp
    h_scr[p] = jnp.where(lo_half, hs[0], hs[1])


def ssd_scan(xs, bc, bt, dt_cols, dt_rows, bias, a_neg, dskip, lay):
    g, tp, gw = xs.shape
    L = SSD_CHUNK
    n_chunks = tp // L
    n_valid = lay.n_valid // L
    hpg = SSD_HEADS_PER_GROUP

    def lane_pad(v):
        return jnp.pad(v.reshape(g, 1, hpg), ((0, 0), (0, 0), (0, LANES - hpg)))

    in_specs, args = [], []
    gps = SSD_GPS
    assert g % gps == 0
    for d in range(2):
        cidx = functools.partial(_chunk_index, lay=lay, reverse=(d == 1), chunk=L)
        in_specs += [pl.BlockSpec((gps, L, gw), lambda gi, k, c=cidx: (gi, c(k), 0)),
                     pl.BlockSpec((gps, L, SSD_STATE), lambda gi, k, c=cidx: (gi, c(k), 0)),
                     pl.BlockSpec((gps, L, SSD_STATE), lambda gi, k, c=cidx: (g // gps + gi, c(k), 0)),
                     pl.BlockSpec((gps, SSD_STATE, L), lambda gi, k, c=cidx: (gi, 0, c(k))),
                     pl.BlockSpec((gps, L, LANES), lambda gi, k, c=cidx: (gi, c(k), 0)),
                     pl.BlockSpec((gps, hpg, L), lambda gi, k, c=cidx: (gi, 0, c(k))),
                     pl.BlockSpec((gps, 1, LANES), lambda gi, k: (gi, 0, 0)),
                     pl.BlockSpec((gps, hpg, 1), lambda gi, k: (gi, 0, 0)),
                     pl.BlockSpec((gps, 1, LANES), lambda gi, k: (gi, 0, 0)),
                     pl.BlockSpec((gps, hpg, 1), lambda gi, k: (gi, 0, 0)),
                     pl.BlockSpec((gps, 1, gw), lambda gi, k: (gi, 0, 0))]
        args += [xs, bc, bc, bt, dt_cols[d], dt_rows[d], lane_pad(bias[d]), bias[d].reshape(g, hpg, 1),
                 lane_pad(a_neg[d]), a_neg[d].reshape(g, hpg, 1),
                 jnp.repeat(dskip[d], SSD_HEAD_DIM).reshape(g, 1, gw)]
    assert len(args) == 2 * N_SSD_IN
    out_specs = [pl.BlockSpec((gps, L, gw),
                              lambda gi, k, c=functools.partial(_chunk_index, lay=lay, reverse=(d == 1), chunk=L):
                              (gi, c(k), 0)) for d in range(2)]
    return pl.pallas_call(
        functools.partial(_ssd_scan_kernel, n_valid_chunks=n_valid),
        grid=(g // gps, n_chunks),
        in_specs=in_specs,
        out_specs=out_specs,
        out_shape=[jax.ShapeDtypeStruct((g, tp, gw), F32)] * 2,
        scratch_shapes=[pltpu.VMEM((gps, hpg // 2, SSD_STATE, LANES), F32)] * 2,
        compiler_params=_cp(("parallel", "arbitrary")),
        name="ssd_scan",
    )(*args)


def _ssd_gate_kernel(yf_ref, yb_ref, z_ref, nw_ref, a_ref):
    gw = yf_ref.shape[-1]
    for g in range(SSD_GROUPS):
        cols = slice(g * gw, (g + 1) * gw)
        gz = (yf_ref[g] + yb_ref[g]) * _silu(z_ref[:, cols])
        gn = gz * lax.rsqrt(jnp.mean(gz * gz, axis=-1, keepdims=True) + NORM_EPS)
        a_ref[:, cols] = (gn * nw_ref[:, cols]).astype(BF16)


def ssd_gate(yf, yb, z, norm_w):
    g, tp, gw = yf.shape
    d_inner = g * gw
    tm = TE
    return pl.pallas_call(
        _ssd_gate_kernel,
        grid=(tp // tm,),
        in_specs=[pl.BlockSpec((g, tm, gw), lambda i: (0, i, 0)),
                  pl.BlockSpec((g, tm, gw), lambda i: (0, i, 0)),
                  pl.BlockSpec((tm, d_inner), lambda i: (i, 0)),
                  pl.BlockSpec((1, d_inner), lambda i: (0, 0))],
        out_specs=pl.BlockSpec((tm, d_inner), lambda i: (i, 0)),
        out_shape=jax.ShapeDtypeStruct((tp, d_inner), BF16),
        compiler_params=_cp(("parallel",)),
        name="ssd_gate",
    )(yf, yb, z, norm_w.reshape(1, d_inner))


def ssd_mixer(h, lay, w_in, conv_w, conv_b, dt_bias, a_log, d_skip, norm_w):
    d_inner = SSD_GROUPS * SSD_HEADS_PER_GROUP * SSD_HEAD_DIM
    gn = SSD_GROUPS * SSD_STATE
    conv_dim = d_inner + 2 * gn
    n_heads = SSD_GROUPS * SSD_HEADS_PER_GROUP
    tp = h.shape[0]
    w16 = w_in.astype(BF16)
    w_rest = w16[:, d_inner:d_inner + conv_dim]
    z = mm(h, w16[:, :d_inner], tn=512, name="ssd_z")
    xs = mm_conv(h, w_rest, conv_w, conv_b, lay, tn=SSD_HEADS_PER_GROUP * SSD_HEAD_DIM, out_dtype=F32,
                 act="silu", gw=SSD_HEADS_PER_GROUP * SSD_HEAD_DIM, col0=0, ncols=d_inner, name="ssd_x")
    bc = mm_conv(h, w_rest, conv_w, conv_b, lay, tn=512, out_dtype=BF16, act="silu",
                 gw=SSD_STATE, col0=d_inner, ncols=2 * gn, name="ssd_bc")
    dt_pre = mm(h, w16[:, d_inner + conv_dim:], name="ssd_dt")
    bt = jnp.transpose(bc[:SSD_GROUPS], (0, 2, 1))
    a_neg = -jnp.exp(a_log.astype(F32))
    dt_cols, dt_rows = [], []
    for d in range(2):
        dt_d = dt_pre[:, d * n_heads:(d + 1) * n_heads].reshape(tp, SSD_GROUPS, SSD_HEADS_PER_GROUP)
        dt_cols.append(jnp.pad(jnp.transpose(dt_d, (1, 0, 2)),
                               ((0, 0), (0, 0), (0, LANES - SSD_HEADS_PER_GROUP))))
        dt_rows.append(jnp.transpose(dt_d, (1, 2, 0)))
    y_fwd, y_rev = ssd_scan(xs, bc, bt, dt_cols, dt_rows, dt_bias, a_neg, d_skip, lay)
    return ssd_gate(y_fwd, y_rev, z, norm_w)


LRU_TB = 128


def _lru_scan_kernel(u_ref, gw_ref, gb_ref, lam_ref, o_ref, carry, *, reverse, n_valid_chunks):
    k = pl.program_id(0)
    tb = LRU_TB
    bw = u_ref.shape[1] // LRU_BLOCKS

    @pl.when(k == 0)
    def _():
        carry[...] = jnp.zeros_like(carry)

    @pl.when(k >= n_valid_chunks)
    def _():
        o_ref[...] = jnp.zeros_like(o_ref)

    @pl.when(k < n_valid_chunks)
    def _():
        u = u_ref[...]
        ub = u.astype(BF16)

        def pre(z):
            parts = [_dot(ub[:, n * bw:(n + 1) * bw], gw_ref[z, n]) for n in range(LRU_BLOCKS)]
            return jnp.concatenate(parts, axis=1) + gb_ref[z]

        r = jax.nn.sigmoid(pre(0))
        gi = jax.nn.sigmoid(pre(1))
        log_a = (-LRU_C) * r * _softplus(-lam_ref[...])
        a = jnp.exp(log_a)
        th = jnp.tanh(log_a)
        b = jnp.sqrt(-2.0 * th / (1.0 - th)) * (gi * u)
        n_g = tb // SUBLANES
        w = a.shape[1]
        sub = lax.broadcasted_iota(jnp.int32, (tb, 1), 0) & (SUBLANES - 1)

        def roll_in_group(x, shift):
            return pltpu.roll(x.reshape(n_g, SUBLANES, w), shift, 1).reshape(tb, w)

        sh = 1
        while sh < SUBLANES:
            if reverse:
                keep = sub < SUBLANES - sh
                a_s = jnp.where(keep, roll_in_group(a, SUBLANES - sh), 1.0)
                b_s = jnp.where(keep, roll_in_group(b, SUBLANES - sh), 0.0)
            else:
                keep = sub >= sh
                a_s = jnp.where(keep, roll_in_group(a, sh), 1.0)
                b_s = jnp.where(keep, roll_in_group(b, sh), 0.0)
            b = a * b_s + b
            a = a * a_s
            sh *= 2
        c = carry[0:1, :]
        for gi in (range(n_g - 1, -1, -1) if reverse else range(n_g)):
            rows = slice(gi * SUBLANES, (gi + 1) * SUBLANES)
            hg = b[rows, :] + a[rows, :] * c
            o_ref[rows, :] = hg
            c = hg[0:1, :] if reverse else hg[SUBLANES - 1:SUBLANES, :]
        carry[0:1, :] = c


def lru_scan(u, gate_w, gate_b, lam, lay, reverse):
    tp, w = u.shape
    tb = LRU_TB
    bw = w // LRU_BLOCKS
    cidx = functools.partial(_chunk_index, lay=lay, reverse=reverse, chunk=tb)
    return pl.pallas_call(
        functools.partial(_lru_scan_kernel, reverse=reverse, n_valid_chunks=lay.n_valid // tb),
        grid=(tp // tb,),
        in_specs=[pl.BlockSpec((tb, w), lambda k: (cidx(k), 0)),
                  pl.BlockSpec((2, LRU_BLOCKS, bw, bw), lambda k: (0, 0, 0, 0)),
                  pl.BlockSpec((2, 1, w), lambda k: (0, 0, 0)),
                  pl.BlockSpec((1, w), lambda k: (0, 0))],
        out_specs=pl.BlockSpec((tb, w), lambda k: (cidx(k), 0)),
        out_shape=jax.ShapeDtypeStruct((tp, w), F32),
        scratch_shapes=[pltpu.VMEM((8, w), F32)],
        compiler_params=_cp(("arbitrary",)),
        name="lru_scan_rev" if reverse else "lru_scan_fwd",
    )(u, gate_w.astype(BF16), gate_b.reshape(2, 1, w), lam.reshape(1, w))


def _lru_gate_kernel(hf_ref, hb_ref, gp_ref, a_ref):
    a_ref[...] = ((hf_ref[...] + hb_ref[...]) * jax.nn.gelu(gp_ref[...], approximate=True)).astype(BF16)


def lru_gate(hf, hb, gate_pre):
    tp, w = hf.shape
    tm = TE
    spec = pl.BlockSpec((tm, w), lambda i: (i, 0))
    return pl.pallas_call(
        _lru_gate_kernel,
        grid=(tp // tm,),
        in_specs=[spec, spec, spec],
        out_specs=spec,
        out_shape=jax.ShapeDtypeStruct((tp, w), BF16),
        compiler_params=_cp(("parallel",)),
        name="lru_gate",
    )(hf, hb, gate_pre)


def lru_mixer(h, lay, w_in, conv_w, conv_b, gate_w, gate_b, lam):
    w = conv_w.shape[1]
    w16 = w_in.astype(BF16)
    gate_pre = mm(h, w16[:, :w], tn=512, name="lru_gate_proj")
    u = mm_conv(h, w16[:, w:], conv_w, conv_b, lay, tn=512, out_dtype=F32, name="lru_x")
    hs = [lru_scan(u, gate_w[d], gate_b[d], lam[d], lay, reverse=(d == 1)) for d in range(2)]
    return lru_gate(hs[0], hs[1], gate_pre)


def rope_tables(lay):
    n = lay.n_lat
    row = jnp.repeat(jnp.arange(n // GRID_W, dtype=F32), GRID_W)
    col = jnp.tile(jnp.arange(GRID_W, dtype=F32), n // GRID_W)
    n_freq = ROPE_DIM // 4
    inv = ROPE_THETA ** (-jnp.arange(n_freq, dtype=F32) / n_freq)
    ang = jnp.concatenate([row[:, None] * inv, col[:, None] * inv], axis=-1)
    cos, sin = jnp.cos(ang), jnp.sin(ang)
    zero = jnp.zeros_like(sin)
    c = jnp.tile(cos, (1, 4))
    s1 = jnp.tile(jnp.concatenate([-sin, zero], axis=1), (1, 2))
    s2 = jnp.tile(jnp.concatenate([zero, sin], axis=1), (1, 2))
    pad = lay.tp - n
    c = jnp.concatenate([c, jnp.ones((pad, LANES), F32)], axis=0)
    s1 = jnp.pad(s1, ((0, pad), (0, 0)))
    s2 = jnp.pad(s2, ((0, pad), (0, 0)))
    return c, s1, s2


MLA_QK_PAD = 256
MLA_HPS = 2


def _mla_q_kernel(a_ref, g_ref, w_ref, c_ref, s1_ref, s2_ref, qr_ref, qp_ref, nscr, *, scale):
    @pl.when(pl.program_id(1) == 0)
    def _():
        x = a_ref[...]
        nscr[...] = (x * lax.rsqrt(jnp.mean(x * x, axis=-1, keepdims=True) + NORM_EPS) * g_ref[...]).astype(BF16)

    acc = _dot(nscr[...], w_ref[...]) * scale
    c, s1, s2 = c_ref[...], s1_ref[...], s2_ref[...]
    for hh in range(MLA_HPS):
        a_h = acc[:, hh * MLA_QK_PAD:(hh + 1) * MLA_QK_PAD]
        qp_ref[hh] = a_h.astype(BF16)
        rot = _rope128(a_h[:, LANES:], c, s1, s2)
        qr_ref[hh] = jnp.concatenate([a_h[:, :LANES], rot], axis=1).astype(BF16)


def _mla_kv_kernel(a_ref, kr_ref, g_ref, w_ref, c_ref, s1_ref, s2_ref, k_ref, v_ref, nscr, krscr):
    @pl.when(pl.program_id(1) == 0)
    def _():
        x = a_ref[...]
        nscr[...] = (x * lax.rsqrt(jnp.mean(x * x, axis=-1, keepdims=True) + NORM_EPS) * g_ref[...]).astype(BF16)
        krscr[...] = _rope128(kr_ref[...], c_ref[...], s1_ref[...], s2_ref[...]).astype(BF16)

    acc = _dot(nscr[...], w_ref[...])
    ones = jnp.ones((acc.shape[0], LANES), BF16)
    for hh in range(MLA_HPS):
        a_h = acc[:, hh * (MLA_NOPE + MLA_V):(hh + 1) * (MLA_NOPE + MLA_V)]
        k_ref[hh] = jnp.concatenate([a_h[:, :MLA_NOPE].astype(BF16), krscr[...]], axis=1)
        v_ref[hh] = jnp.concatenate([a_h[:, MLA_NOPE:].astype(BF16), ones], axis=1)


MLA_TQ = 1024
MLA_TK = 512
MLA_SUB = 256


def _mla_ctx_attn_kernel(qp_ref, k_ref, v_ref, o_ref):
    s = _dot_nt(qp_ref[...], k_ref[...])
    p = jnp.exp2(s - jnp.max(s, axis=-1, keepdims=True)).astype(BF16)
    acc = _dot(p, v_ref[...])
    o_ref[...] = (acc[:, :MLA_V] / acc[:, MLA_V:]).astype(o_ref.dtype)


def _mla_attn_kernel(qr_ref, qp_ref, k_ref, v_ref, o_ref, acc_ref, s_ref, *, n_lat, n_ctx):
    tq, tk, sub = MLA_TQ, MLA_TK, MLA_SUB
    nsub = tq // sub
    n_kt = n_lat // tk
    assert n_kt % 2 == 0
    rows = [slice(c * sub, (c + 1) * sub) for c in range(nsub)]

    def qk_lat(c, tile):
        start = pl.multiple_of(tile * tk, tk)
        return _dot_nt(qr_ref[rows[c], :], k_ref[pl.ds(start, tk), :])

    def consume(c, s, vv, m):
        m_new = jnp.maximum(m, jnp.max(s, axis=-1, keepdims=True))
        alpha = jnp.exp2(m - m_new)
        p = jnp.exp2(s - m_new).astype(BF16)
        acc_ref[rows[c], :] = alpha * acc_ref[rows[c], :] + _dot(p, vv)
        return m_new

    def v_tile(tile):
        return v_ref[pl.ds(pl.multiple_of(tile * tk, tk), tk), :]

    def half(ms, slot, tile):
        vv = v_tile(tile)
        out = []
        for c in range(nsub):
            s = s_ref[slot, rows[c], :]
            s_ref[1 - slot, rows[c], :] = qk_lat(c, tile + 1)
            out.append(consume(c, s, vv, ms[c]))
        return tuple(out)

    def finish():
        acc = acc_ref[...]
        o_ref[...] = (acc[:, :MLA_V] / acc[:, MLA_V:]).astype(o_ref.dtype)

    acc_ref[...] = jnp.zeros_like(acc_ref)
    m0 = tuple(jnp.full((sub, 1), -jnp.inf, F32) for _ in range(nsub))
    k_ctx = lambda: k_ref[pl.ds(n_lat, n_ctx), :]
    v_ctx = lambda: v_ref[pl.ds(n_lat, n_ctx), :]

    for c in range(nsub):
        s_ref[0, rows[c], :] = qk_lat(c, 0)

    def pair(pi, ms):
        ms = half(ms, 0, 2 * pi)
        return half(ms, 1, 2 * pi + 1)

    ms = lax.fori_loop(0, n_kt // 2 - 1, pair, m0)
    ms = half(ms, 0, n_kt - 2)
    vv = v_tile(n_kt - 1)
    for c in range(nsub):
        s = s_ref[1, rows[c], :]
        s_c = _dot_nt(qp_ref[rows[c], :], k_ctx())
        m = consume(c, s, vv, ms[c])
        consume(c, s_c, v_ctx(), m)
    finish()


def mla_mixer(h, lay, tables, w_in, q_norm, kv_norm, w_q_up, w_kv_up):
    tp = h.shape[0]
    d = h.shape[1]
    c, s1, s2 = tables
    qk = MLA_NOPE + ROPE_DIM
    w_lat = jnp.concatenate([
        w_in[:, :MLA_Q_RANK], jnp.zeros((d, 256), F32),
        w_in[:, MLA_Q_RANK:MLA_Q_RANK + MLA_KV_RANK],
        w_in[:, MLA_Q_RANK + MLA_KV_RANK:], jnp.zeros((d, LANES - ROPE_DIM), F32)], axis=1).astype(BF16)
    lat = mm(h, w_lat, name="mla_lat")
    wq = jnp.pad(w_q_up.reshape(MLA_Q_RANK, MLA_HEADS, qk),
                 ((0, 0), (0, 0), (0, MLA_QK_PAD - qk))).reshape(MLA_Q_RANK, MLA_HEADS * MLA_QK_PAD).astype(BF16)
    n_t = tp // TM
    tab_spec = pl.BlockSpec((TM, LANES), lambda i, j: (i, 0))
    q_rot, q_plain = pl.pallas_call(
        functools.partial(_mla_q_kernel, scale=qk ** -0.5 * LOG2E),
        grid=(n_t, MLA_HEADS // MLA_HPS),
        in_specs=[pl.BlockSpec((TM, MLA_Q_RANK), lambda i, j: (i, 0)),
                  pl.BlockSpec((1, MLA_Q_RANK), lambda i, j: (0, 0)),
                  pl.BlockSpec((MLA_Q_RANK, MLA_HPS * MLA_QK_PAD), lambda i, j: (0, j)),
                  tab_spec, tab_spec, tab_spec],
        out_specs=[pl.BlockSpec((MLA_HPS, TM, MLA_QK_PAD), lambda i, j: (j, i, 0)),
                   pl.BlockSpec((MLA_HPS, TM, MLA_QK_PAD), lambda i, j: (j, i, 0))],
        out_shape=[jax.ShapeDtypeStruct((MLA_HEADS, tp, MLA_QK_PAD), BF16)] * 2,
        scratch_shapes=[pltpu.VMEM((TM, MLA_Q_RANK), BF16)],
        compiler_params=_cp(("parallel", "arbitrary")),
        name="mla_q",
    )(lat, q_norm.reshape(1, -1), wq, c, s1, s2)
    kv_col = (MLA_Q_RANK + 256) // MLA_KV_RANK
    kr_col = (MLA_Q_RANK + 256 + MLA_KV_RANK) // LANES
    k_all, v_all = pl.pallas_call(
        _mla_kv_kernel,
        grid=(n_t, MLA_HEADS // MLA_HPS),
        in_specs=[pl.BlockSpec((TM, MLA_KV_RANK), lambda i, j: (i, kv_col)),
                  pl.BlockSpec((TM, LANES), lambda i, j: (i, kr_col)),
                  pl.BlockSpec((1, MLA_KV_RANK), lambda i, j: (0, 0)),
                  pl.BlockSpec((MLA_KV_RANK, MLA_HPS * (MLA_NOPE + MLA_V)), lambda i, j: (0, j)),
                  tab_spec, tab_spec, tab_spec],
        out_specs=[pl.BlockSpec((MLA_HPS, TM, MLA_QK_PAD), lambda i, j: (j, i, 0)),
                   pl.BlockSpec((MLA_HPS, TM, MLA_V + LANES), lambda i, j: (j, i, 0))],
        out_shape=[jax.ShapeDtypeStruct((MLA_HEADS, tp, MLA_QK_PAD), BF16),
                   jax.ShapeDtypeStruct((MLA_HEADS, tp, MLA_V + LANES), BF16)],
        scratch_shapes=[pltpu.VMEM((TM, MLA_KV_RANK), BF16), pltpu.VMEM((TM, LANES), BF16)],
        compiler_params=_cp(("parallel", "arbitrary")),
        name="mla_kv",
    )(lat, lat, kv_norm.reshape(1, -1), w_kv_up.astype(BF16), c, s1, s2)
    tq = min(MLA_TQ, lay.n_lat)
    assert lay.n_lat % tq == 0 and tq % MLA_SUB == 0
    o_lat = pl.pallas_call(
        functools.partial(_mla_attn_kernel, n_lat=lay.n_lat, n_ctx=lay.n_ctx),
        grid=(MLA_HEADS, lay.n_lat // tq),
        in_specs=[pl.BlockSpec((None, tq, MLA_QK_PAD), lambda hh, i: (hh, i, 0)),
                  pl.BlockSpec((None, tq, MLA_QK_PAD), lambda hh, i: (hh, i, 0)),
                  pl.BlockSpec((None, tp, MLA_QK_PAD), lambda hh, i: (hh, 0, 0)),
                  pl.BlockSpec((None, tp, MLA_V + LANES), lambda hh, i: (hh, 0, 0))],
        out_specs=pl.BlockSpec((tq, MLA_V), lambda hh, i: (i, hh)),
        out_shape=jax.ShapeDtypeStruct((lay.n_lat, MLA_HEADS * MLA_V), BF16),
        scratch_shapes=[pltpu.VMEM((tq, MLA_V + LANES), F32), pltpu.VMEM((2, tq, MLA_TK), F32)],
        compiler_params=_cp(("parallel", "parallel")),
        name="mla_attn",
    )(q_rot, q_plain, k_all, v_all)
    ctx_rows = tp - lay.n_lat
    ctx_blk = lay.n_lat // lay.n_ctx
    o_ctx = pl.pallas_call(
        _mla_ctx_attn_kernel,
        grid=(MLA_HEADS,),
        in_specs=[pl.BlockSpec((None, ctx_rows, MLA_QK_PAD), lambda hh: (hh, lay.n_lat // ctx_rows, 0)),
                  pl.BlockSpec((None, lay.n_ctx, MLA_QK_PAD), lambda hh: (hh, ctx_blk, 0)),
                  pl.BlockSpec((None, lay.n_ctx, MLA_V + LANES), lambda hh: (hh, ctx_blk, 0))],
        out_specs=pl.BlockSpec((ctx_rows, MLA_V), lambda hh: (0, hh)),
        out_shape=jax.ShapeDtypeStruct((ctx_rows, MLA_HEADS * MLA_V), BF16),
        compiler_params=_cp(("parallel",)),
        name="mla_attn_ctx",
    )(q_plain, k_all, v_all)
    return jnp.concatenate([o_lat, o_ctx], axis=0)


def _mm_bias_rope_kernel(*refs, scale, rope, emit_plain):
    a_ref, w_ref, b_ref = refs[:3]
    if rope:
        c_ref, s1_ref, s2_ref = refs[3:6]
        outs = refs[6:]
    else:
        outs = refs[3:]
    acc = (_dot(a_ref[...], w_ref[...]) + b_ref[...]) * scale
    if not rope:
        outs[0][...] = acc.astype(outs[0].dtype)
        return
    c, s1, s2 = c_ref[...], s1_ref[...], s2_ref[...]
    tn = acc.shape[1]
    for s in range(tn // LANES):
        cols = slice(s * LANES, (s + 1) * LANES)
        outs[0][:, cols] = _rope128(acc[:, cols], c, s1, s2).astype(outs[0].dtype)
    if emit_plain:
        outs[1][...] = acc.astype(outs[1].dtype)


def mm_bias_rope(a, w, b, tables, *, scale=1.0, rope=True, emit_plain=False, name="mm_bias_rope"):
    m, k = a.shape
    n = w.shape[1]
    tn = 512
    tm = m // RESID_ROW_TILES
    assert m % RESID_ROW_TILES == 0 and tm % BF16_ROWS == 0
    in_specs = [pl.BlockSpec((tm, k), lambda i, j: (i, 0)),
                pl.BlockSpec((k, tn), lambda i, j: (0, j)),
                pl.BlockSpec((1, tn), lambda i, j: (0, j))]
    args = [a, w, b.reshape(1, n)]
    if rope:
        in_specs += [pl.BlockSpec((tm, LANES), lambda i, j: (i, 0))] * 3
        args += list(tables)
    n_out = 2 if (rope and emit_plain) else 1
    out_spec = pl.BlockSpec((tm, tn), lambda i, j: (i, j))
    res = pl.pallas_call(
        functools.partial(_mm_bias_rope_kernel, scale=scale, rope=rope, emit_plain=emit_plain),
        grid=(RESID_ROW_TILES, n // tn),
        in_specs=in_specs,
        out_specs=[out_spec] * n_out,
        out_shape=[jax.ShapeDtypeStruct((m, n), BF16)] * n_out,
        compiler_params=_cp(("parallel", "parallel")),
        name=name,
    )(*args)
    return res if n_out == 2 else res[0]


SWA_BPS = 2


def _swa_attn_kernel(qr_ref, qp_ref, kp_ref, kc_ref, kn_ref, vp_ref, vc_ref, vn_ref,
                     kctx_ref, vctx_ref, sink_ref, o_ref, *, n_lat):
    b = pl.program_id(1)
    blk = ATTN_BLOCK
    nb = n_lat // blk

    @pl.when(b * SWA_BPS >= nb)
    def _():
        o_ref[...] = jnp.zeros_like(o_ref)

    @pl.when(b * SWA_BPS < nb)
    def _():
        lane = lax.broadcasted_iota(jnp.int32, (blk, LANES), 1)
        r = lax.broadcasted_iota(jnp.int32, (blk, blk), 0)
        c = lax.broadcasted_iota(jnp.int32, (blk, blk), 1)
        lo_half = lane < SWA_HEAD_DIM
        kwin = jnp.concatenate([kp_ref[...], kc_ref[...], kn_ref[...]], axis=0)
        vwin = jnp.concatenate([vp_ref[...], vc_ref[...], vn_ref[...]], axis=0)
        ones_w = jnp.ones(vwin.shape, BF16)
        vwe = jnp.concatenate([vwin, ones_w], axis=1)
        vce = jnp.concatenate([vctx_ref[...], ones_w[:vctx_ref.shape[0]]], axis=1)
        kctx = kctx_ref[...]
        n_c = kctx.shape[0]
        scores = []
        for t in range(SWA_BPS):
            gb = b * SWA_BPS + t
            rows = slice(t * blk, (t + 1) * blk)
            win = slice(t * blk, (t + 3) * blk)
            bias_p = jnp.where((c >= r) & (gb > 0), 0.0, -jnp.inf)
            bias_n = jnp.where((c <= r) & (gb < nb - 1), 0.0, -jnp.inf)
            for g in range(SWA_GROUP):
                cols = slice((g // 2) * LANES, (g // 2 + 1) * LANES)
                keep = lo_half if g % 2 == 0 else jnp.logical_not(lo_half)
                qr = qr_ref[rows, cols]
                qp = qp_ref[rows, cols]
                qr = jnp.where(keep, qr, jnp.zeros_like(qr))
                qp = jnp.where(keep, qp, jnp.zeros_like(qp))
                s_c = _dot_nt(qp, kctx)
                s_w = _dot_nt(qr, kwin[win])
                scores.append(jnp.concatenate([s_c, s_w[:, :blk] + bias_p, s_w[:, blk:2 * blk],
                                               s_w[:, 2 * blk:] + bias_n], axis=1))
        for t in range(SWA_BPS):
            rows = slice(t * blk, (t + 1) * blk)
            win = slice(t * blk, (t + 3) * blk)
            outs = []
            for g in range(SWA_GROUP):
                s = scores[t * SWA_GROUP + g]
                sk = sink_ref[g:g + 1, 0:1] * LOG2E
                m = jnp.maximum(jnp.max(s, axis=-1, keepdims=True), sk)
                p = jnp.exp2(s - m).astype(BF16)
                oe = _dot(p[:, :n_c], vce) + _dot(p[:, n_c:], vwe[win])
                outs.append(oe[:, :LANES] / (oe[:, LANES:] + jnp.exp2(sk - m)))
            o_ref[rows, :] = jnp.concatenate(
                [jnp.where(lo_half, outs[0], outs[1]), jnp.where(lo_half, outs[2], outs[3])],
                axis=1).astype(o_ref.dtype)


def swa_mixer(h, lay, tables, w_qkv, b_qkv, sink):
    tp = h.shape[0]
    d = h.shape[1]
    hd = SWA_HEAD_DIM
    qd = SWA_Q_HEADS * hd
    kd = SWA_KV_HEADS * hd

    def dup_heads(wm):
        lead = wm.shape[:-1]
        w3 = wm.reshape(lead + (SWA_KV_HEADS, 1, hd))
        return jnp.broadcast_to(w3, lead + (SWA_KV_HEADS, 2, hd)).reshape(lead + (2 * kd,))

    w16 = w_qkv.astype(BF16)
    q_rot, q_plain = mm_bias_rope(h, w16[:, :qd], b_qkv[:qd], tables, scale=hd ** -0.5 * LOG2E,
                                  emit_plain=True, name="swa_q")
    k_dup = mm_bias_rope(h, dup_heads(w16[:, qd:qd + kd]), dup_heads(b_qkv[qd:qd + kd]), tables, name="swa_k")
    v_dup = mm_bias_rope(h, dup_heads(w16[:, qd + kd:]), dup_heads(b_qkv[qd + kd:]), tables,
                         rope=False, name="swa_v")
    blk = ATTN_BLOCK
    nb = lay.n_lat // blk
    ctx_blk = lay.n_lat // lay.n_ctx
    sink_t = jnp.broadcast_to(jnp.pad(sink.astype(F32).reshape(SWA_KV_HEADS, SWA_GROUP),
                                      ((0, 0), (0, 8 - SWA_GROUP)))[:, :, None], (SWA_KV_HEADS, 8, LANES))
    qw = SWA_GROUP * hd
    bps = SWA_BPS
    assert nb % bps == 0 and tp % (blk * bps) == 0
    q_spec = pl.BlockSpec((blk * bps, qw), lambda hh, b: (b, hh))
    prev_spec = pl.BlockSpec((blk, LANES), lambda hh, b: (jnp.clip(b * bps - 1, 0, nb - 1), hh))
    cur_spec = pl.BlockSpec((blk * bps, LANES), lambda hh, b: (jnp.minimum(b, nb // bps - 1), hh))
    next_spec = pl.BlockSpec((blk, LANES), lambda hh, b: (jnp.clip(b * bps + bps, 0, nb - 1), hh))
    ctx_spec = pl.BlockSpec((lay.n_ctx, LANES), lambda hh, b: (ctx_blk, hh))
    return pl.pallas_call(
        functools.partial(_swa_attn_kernel, n_lat=lay.n_lat),
        grid=(SWA_KV_HEADS, tp // (blk * bps)),
        in_specs=[q_spec, q_spec, prev_spec, cur_spec, next_spec,
                  prev_spec, cur_spec, next_spec, ctx_spec, ctx_spec,
                  pl.BlockSpec((None, 8, LANES), lambda hh, b: (hh, 0, 0))],
        out_specs=pl.BlockSpec((blk * bps, qw), lambda hh, b: (b, hh)),
        out_shape=jax.ShapeDtypeStruct((tp, qd), BF16),
        compiler_params=_cp(("parallel", "parallel")),
        name="swa_attn",
    )(q_rot, q_plain, k_dup, k_dup, k_dup, v_dup, v_dup, v_dup, k_dup, v_dup, sink_t)


def conv_ffn_hidden(h, lay, w_up, conv_w, conv_b):
    hidden = w_up.shape[1] // 2
    return mm_conv(h, w_up.astype(BF16), conv_w, conv_b, lay, tn=512, out_dtype=BF16, gated=True,
                   ncols=hidden, name="ffn_up")


def kernel(x, c, ctx, c_ctx, ada_w, ada_b, norm_mix, norm_ffn, ffn_up, ffn_conv_w, ffn_conv_b, ffn_down, final_norm, ssd_in, ssd_conv_w, ssd_conv_b, ssd_dt_bias, ssd_a_log, ssd_d, ssd_norm, ssd_out, lru_in, lru_conv_w, lru_conv_b, lru_gate_w, lru_gate_b, lru_lambda, lru_out, mla_in, mla_q_norm, mla_kv_norm, mla_q_up, mla_kv_up, mla_out, swa_qkv, swa_qkv_b, swa_sink, swa_out):
    bsz, n_lat, d = x.shape
    n_ctx = ctx.shape[1]
    depth = ada_w.shape[0]
    assert bsz == 1 and depth == 4, "one sample, four layers (one of each mixer)"
    lay = make_layout(n_lat, n_ctx)
    xs = jnp.concatenate([x[0], ctx[0], jnp.zeros((lay.tp - lay.n_valid, d), x.dtype)], axis=0)
    cv = jnp.concatenate([c, c_ctx[None, :], jnp.zeros((6, d), F32)], axis=0)
    mods = adaln(cv, ada_w, ada_b)
    tables = rope_tables(lay)
    for i in range(depth):
        mod = jnp.pad(mods[i, :2].reshape(2, ADA_CHUNKS, d), ((0, 0), (0, 8 - ADA_CHUNKS), (0, 0)))
        h = norm_mod(xs, norm_mix[i], mod, 0, 1, lay)
        if i == 0:
            a = ssd_mixer(h, lay, ssd_in[0], ssd_conv_w[0], ssd_conv_b[0], ssd_dt_bias[0], ssd_a_log[0],
                          ssd_d[0], ssd_norm[0])
            w_o = ssd_out[0]
        elif i == 1:
            a = lru_mixer(h, lay, lru_in[0], lru_conv_w[0], lru_conv_b[0], lru_gate_w[0], lru_gate_b[0],
                          lru_lambda[0])
            w_o = lru_out[0]
        elif i == 2:
            a = mla_mixer(h, lay, tables, mla_in[0], mla_q_norm[0], mla_kv_norm[0], mla_q_up[0], mla_kv_up[0])
            w_o = mla_out[0]
        else:
            a = swa_mixer(h, lay, tables, swa_qkv[0], swa_qkv_b[0], swa_sink[0])
            w_o = swa_out[0]
        xs = mm_resid(a, w_o.astype(BF16), xs, mod, 2, lay, name="mix_out")
        h = norm_mod(xs, norm_ffn[i], mod, 3, 4, lay)
        hid = conv_ffn_hidden(h, lay, ffn_up[i], ffn_conv_w[i], ffn_conv_b[i])
        xs = mm_resid(hid, ffn_down[i].astype(BF16), xs, mod, 5, lay, name="ffn_down")
    return rmsnorm_rows(xs, final_norm, n_lat)[None]
```

```python
import functools
import math
from typing import NamedTuple

import jax
import jax.numpy as jnp
from jax import lax
from jax.experimental import pallas as pl
from jax.experimental.pallas import tpu as pltpu

F32 = jnp.float32
BF16 = jnp.bfloat16

GRID_W = 64
NORM_EPS = 1e-6
ADA_CHUNKS = 6
ROPE_THETA = 10000.0
ROPE_DIM = 64
SSD_HEAD_DIM = 64
SSD_GROUPS = 8
SSD_HEADS_PER_GROUP = 8
SSD_STATE = 128
SSD_CHUNK = 128
LRU_BLOCKS = 8
LRU_C = 8.0
MLA_HEADS = 16
MLA_Q_RANK = 768
MLA_KV_RANK = 512
MLA_NOPE = 128
MLA_V = 128
SWA_Q_HEADS = 32
SWA_KV_HEADS = 8
SWA_GROUP = 4
SWA_HEAD_DIM = 64
SWA_WINDOW = 128
ATTN_BLOCK = 128

LANES = 128
SUBLANES = 8
MXU_COLS = 256
BF16_ROWS = 16
TM = 512
TE = 256
RESID_ROW_TILES = 8
VMEM_LIMIT = 56 * 2 ** 20
LOG2E = math.log2(math.e)


class Layout(NamedTuple):
    n_lat: int
    n_ctx: int
    tp: int

    @property
    def n_valid(self):
        return self.n_lat + self.n_ctx


def make_layout(n_lat, n_ctx):
    assert n_lat % TM == 0 and n_ctx % SSD_CHUNK == 0
    ctx_pad = -(-n_ctx // TM) * TM
    return Layout(n_lat, n_ctx, n_lat + ctx_pad)


def _cp(sem):
    return pltpu.CompilerParams(dimension_semantics=sem, vmem_limit_bytes=VMEM_LIMIT)


def _silu(x):
    return x * jax.nn.sigmoid(x)


def _softplus(x):
    return jnp.maximum(x, 0.0) + jnp.log1p(jnp.exp(-jnp.abs(x)))


def _dot(a, b):
    return jnp.dot(a, b, preferred_element_type=F32)


def _dot_nt(a, b):
    return lax.dot_general(a, b, (((1,), (1,)), ((), ())), preferred_element_type=F32)


def _split3(x):
    hi = x.astype(BF16)
    r = x - hi.astype(F32)
    mid = r.astype(BF16)
    lo = (r - mid.astype(F32)).astype(BF16)
    return hi, mid, lo


def _rope128(v, c, s1, s2):
    return v * c + pltpu.roll(v, 96, 1) * s1 + pltpu.roll(v, 32, 1) * s2


def _adaln_kernel(cv_ref, w_ref, b_ref, o_ref):
    s = _silu(cv_ref[...]).astype(BF16)
    o_ref[...] = _dot(s, w_ref[...].astype(BF16)) + b_ref[...]


def adaln(cv, ada_w, ada_b):
    depth, d, n = ada_w.shape
    tn = 1024
    return pl.pallas_call(
        _adaln_kernel,
        grid=(depth, n // tn),
        in_specs=[pl.BlockSpec((8, d), lambda l, j: (0, 0)),
                  pl.BlockSpec((None, d, tn), lambda l, j: (l, 0, j)),
                  pl.BlockSpec((None, 1, tn), lambda l, j: (l, 0, j))],
        out_specs=pl.BlockSpec((None, 8, tn), lambda l, j: (l, 0, j)),
        out_shape=jax.ShapeDtypeStruct((depth, 8, n), F32),
        compiler_params=_cp(("parallel", "parallel")),
        name="adaln",
    )(cv, ada_w, ada_b.reshape(depth, 1, n))


def _norm_mod_kernel(x_ref, g_ref, mod_ref, h_ref, *, shift_idx, scale_idx, n_valid, tm):
    x = x_ref[...]
    y = x * lax.rsqrt(jnp.mean(x * x, axis=-1, keepdims=True) + NORM_EPS) * g_ref[...]
    h = y * (1.0 + mod_ref[scale_idx:scale_idx + 1, :]) + mod_ref[shift_idx:shift_idx + 1, :]
    row = pl.program_id(0) * tm + lax.broadcasted_iota(jnp.int32, (tm, 1), 0)
    h_ref[...] = jnp.where(row < n_valid, h, 0.0).astype(BF16)


def norm_mod(x, g, mod, shift_idx, scale_idx, lay):
    tp, d = x.shape
    tm = TM
    return pl.pallas_call(
        functools.partial(_norm_mod_kernel, shift_idx=shift_idx, scale_idx=scale_idx,
                          n_valid=lay.n_valid, tm=tm),
        grid=(tp // tm,),
        in_specs=[pl.BlockSpec((tm, d), lambda i: (i, 0)),
                  pl.BlockSpec((1, d), lambda i: (0, 0)),
                  pl.BlockSpec((None, 8, d), lambda i: ((i * tm >= lay.n_lat).astype(jnp.int32), 0, 0))],
        out_specs=pl.BlockSpec((tm, d), lambda i: (i, 0)),
        out_shape=jax.ShapeDtypeStruct((tp, d), BF16),
        compiler_params=_cp(("parallel",)),
        name="norm_mod",
    )(x, g.reshape(1, d), mod)


def _rmsnorm_kernel(x_ref, g_ref, o_ref):
    x = x_ref[...]
    o_ref[...] = x * lax.rsqrt(jnp.mean(x * x, axis=-1, keepdims=True) + NORM_EPS) * g_ref[...]


def rmsnorm_rows(x, g, rows):
    d = x.shape[1]
    tm = TM
    return pl.pallas_call(
        _rmsnorm_kernel,
        grid=(rows // tm,),
        in_specs=[pl.BlockSpec((tm, d), lambda i: (i, 0)), pl.BlockSpec((1, d), lambda i: (0, 0))],
        out_specs=pl.BlockSpec((tm, d), lambda i: (i, 0)),
        out_shape=jax.ShapeDtypeStruct((rows, d), F32),
        compiler_params=_cp(("parallel",)),
        name="final_norm",
    )(x, g.reshape(1, d))


def _mm_kernel(a_ref, w_ref, o_ref):
    o_ref[...] = _dot(a_ref[...], w_ref[...]).astype(o_ref.dtype)


def mm(a, w, out_dtype=F32, tn=None, name="mm"):
    m, k = a.shape
    n = w.shape[1]
    tn = n if tn is None else tn
    tm = m // RESID_ROW_TILES
    assert m % RESID_ROW_TILES == 0 and tm % BF16_ROWS == 0
    return pl.pallas_call(
        _mm_kernel,
        grid=(RESID_ROW_TILES, n // tn),
        in_specs=[pl.BlockSpec((tm, k), lambda i, j: (i, 0)),
                  pl.BlockSpec((k, tn), lambda i, j: (0, j))],
        out_specs=pl.BlockSpec((tm, tn), lambda i, j: (i, j)),
        out_shape=jax.ShapeDtypeStruct((m, n), out_dtype),
        compiler_params=_cp(("parallel", "parallel")),
        name=name,
    )(a, w)


def _mm_resid_kernel(a_ref, w_ref, x_ref, mod_ref, o_ref, *, gate_idx, n_lat, tm):
    acc = _dot(a_ref[...], w_ref[...])
    row = pl.program_id(0) * tm + lax.broadcasted_iota(jnp.int32, (tm, 1), 0)
    gate = jnp.where(row < n_lat, mod_ref[0, gate_idx:gate_idx + 1, :], mod_ref[1, gate_idx:gate_idx + 1, :])
    o_ref[...] = x_ref[...] + gate * acc


def mm_resid(a, w, x, mod, gate_idx, lay, name="mm_resid"):
    m, k = a.shape
    n = w.shape[1]
    tn = 512
    tm = m // RESID_ROW_TILES
    assert m % RESID_ROW_TILES == 0 and tm % BF16_ROWS == 0
    return pl.pallas_call(
        functools.partial(_mm_resid_kernel, gate_idx=gate_idx, n_lat=lay.n_lat, tm=tm),
        grid=(RESID_ROW_TILES, n // tn),
        in_specs=[pl.BlockSpec((tm, k), lambda i, j: (i, 0)),
                  pl.BlockSpec((k, tn), lambda i, j: (0, j)),
                  pl.BlockSpec((tm, tn), lambda i, j: (i, j)),
                  pl.BlockSpec((2, 8, tn), lambda i, j: (0, 0, j))],
        out_specs=pl.BlockSpec((tm, tn), lambda i, j: (i, j)),
        out_shape=jax.ShapeDtypeStruct((m, n), F32),
        compiler_params=_cp(("parallel", "parallel")),
        name=name,
    )(a, w, x, mod)


HALO = BF16_ROWS


def _mm_conv_kernel(*refs, ksize, gated, act, tm, n_lat_tiles, n_tiles, gw):
    nw = 2 if gated else 1
    a_ref, ap_ref, an_ref = refs[:3]
    w_refs = refs[3:3 + nw]
    cw_refs = refs[3 + nw:3 + 2 * nw]
    cb_refs = refs[3 + 2 * nw:3 + 3 * nw]
    o_ref = refs[3 + 3 * nw]
    hbuf = refs[4 + 3 * nw]
    u_scrs = refs[5 + 3 * nw:]
    i = pl.program_id(0)
    tn = w_refs[0].shape[1]

    @pl.when(pl.program_id(1) == 0)
    def _():
        first = (i == 0) | (i == n_lat_tiles)
        last = (i == n_lat_tiles - 1) | (i == n_tiles - 1)
        prev = ap_ref[...]
        nxt = an_ref[...]
        hbuf[0:HALO, :] = jnp.where(first, jnp.zeros_like(prev), prev)
        hbuf[HALO:HALO + tm, :] = a_ref[...]
        hbuf[HALO + tm:, :] = jnp.where(last, jnp.zeros_like(nxt), nxt)

    left = (ksize - 1) // 2

    def conv(wi):
        u_scrs[wi][...] = _dot(hbuf[...], w_refs[wi][...])
        acc = cb_refs[wi][...] + cw_refs[wi][0:1, :] * u_scrs[wi][pl.ds(HALO - left, tm), :]
        for kk in range(1, ksize):
            acc = acc + cw_refs[wi][kk:kk + 1, :] * u_scrs[wi][pl.ds(HALO - left + kk, tm), :]
        return acc

    if gated:
        out = _silu(conv(0)) * conv(1)
    else:
        out = conv(0)
        if act == "silu":
            out = _silu(out)
    out = out.astype(o_ref.dtype)
    if gw is None:
        o_ref[...] = out
    else:
        for q in range(tn // gw):
            o_ref[q] = out[:, q * gw:(q + 1) * gw]


def mm_conv(a, w, conv_w, conv_b, lay, *, tn, out_dtype, gated=False, act=None,
            gw=None, col0=0, ncols=None, name="mm_conv"):
    tp, k = a.shape
    ksize = conv_w.shape[0]
    ncols = ncols if ncols is not None else w.shape[1]
    nj = ncols // tn
    c0 = col0 // tn
    assert col0 % tn == 0 and ncols % tn == 0
    assert gw is None or tn % gw == 0
    n_tiles = tp // TM
    hb = TM // HALO
    a_specs = [pl.BlockSpec((TM, k), lambda i, j: (i, 0)),
               pl.BlockSpec((HALO, k), lambda i, j: (jnp.maximum(i * hb - 1, 0), 0)),
               pl.BlockSpec((HALO, k), lambda i, j: (jnp.minimum((i + 1) * hb, tp // HALO - 1), 0))]
    offs = [c0, c0 + nj] if gated else [c0]
    w_specs = [pl.BlockSpec((k, tn), lambda i, j, o=o: (0, o + j)) for o in offs]
    cw_specs = [pl.BlockSpec((ksize, tn), lambda i, j, o=o: (0, o + j)) for o in offs]
    cb_specs = [pl.BlockSpec((1, tn), lambda i, j, o=o: (0, o + j)) for o in offs]
    nw = len(offs)
    if gw is not None:
        out_spec = pl.BlockSpec((tn // gw, TM, gw), lambda i, j: (j, i, 0))
        out_shape = jax.ShapeDtypeStruct((ncols // gw, tp, gw), out_dtype)
    else:
        out_spec = pl.BlockSpec((TM, tn), lambda i, j: (i, j))
        out_shape = jax.ShapeDtypeStruct((tp, ncols), out_dtype)
    cb2 = conv_b.reshape(1, -1)
    return pl.pallas_call(
        functools.partial(_mm_conv_kernel, ksize=ksize, gated=gated, act=act, tm=TM,
                          n_lat_tiles=lay.n_lat // TM, n_tiles=n_tiles, gw=gw),
        grid=(n_tiles, nj),
        in_specs=a_specs + w_specs + cw_specs + cb_specs,
        out_specs=out_spec,
        out_shape=out_shape,
        scratch_shapes=[pltpu.VMEM((TM + 2 * HALO, k), BF16)]
        + [pltpu.VMEM((TM + 2 * HALO, tn), F32) for _ in range(nw)],
        compiler_params=_cp(("parallel", "arbitrary")),
        name=name,
    )(a, a, a, *([w] * nw), *([conv_w] * nw), *([cb2] * nw))


def _chunk_index(k, lay, reverse, chunk):
    n_lc = lay.n_lat // chunk
    n_cc = lay.n_ctx // chunk
    if reverse:
        valid = n_lc + n_cc - 1 - k
    else:
        valid = jnp.where(k < n_cc, n_lc + k, k - n_cc)
    return jnp.where(k < n_lc + n_cc, valid, k)


N_SSD_IN = 11
SSD_GPS = 4


def _ssd_scan_kernel(*refs, n_valid_chunks):
    fwd_in, rev_in = refs[:N_SSD_IN], refs[N_SSD_IN:2 * N_SSD_IN]
    yf_ref, yb_ref, hf_scr, hb_scr = refs[2 * N_SSD_IN:]
    k = pl.program_id(1)

    @pl.when(k == 0)
    def _():
        hf_scr[...] = jnp.zeros_like(hf_scr)
        hb_scr[...] = jnp.zeros_like(hb_scr)

    @pl.when(k >= n_valid_chunks)
    def _():
        yf_ref[...] = jnp.zeros_like(yf_ref)
        yb_ref[...] = jnp.zeros_like(yb_ref)

    @pl.when(k < n_valid_chunks)
    def _():
        chains = []
        for gi in range(SSD_GPS):
            for ins, y_ref, h_scr, rev in ((fwd_in, yf_ref, hf_scr, False), (rev_in, yb_ref, hb_scr, True)):
                views = [r.at[gi] for r in ins]
                chains.append((_ssd_chunk_setup(*views, reverse=rev), views[0], views[-1],
                               y_ref.at[gi], h_scr.at[gi]))
        for p in range(SSD_HEADS_PER_GROUP // 2):
            for pre, xs_v, dsk_v, y_v, h_v in chains:
                _ssd_head_pair(p, pre, xs_v, dsk_v, y_v, h_v)


def _ssd_chunk_setup(xs_ref, b_ref, c_ref, bt_ref, dtc_ref, dtr_ref, biasc_ref, biasr_ref,
                     ac_ref, ar_ref, dsk_ref, *, reverse):
    L = SSD_CHUNK
    li = lax.broadcasted_iota(jnp.int32, (L, L), 0)
    si = lax.broadcasted_iota(jnp.int32, (L, L), 1)
    if reverse:
        mask = si >= li
        mask_t = li >= si
    else:
        mask = si <= li
        mask_t = li <= si
    tri = jnp.where(mask, 1.0, 0.0).astype(BF16)
    tri_t = jnp.where(mask_t, 1.0, 0.0).astype(BF16)

    dtc = _softplus(dtc_ref[...] + biasc_ref[...])
    adt_c = dtc * ac_ref[...]
    hi, mid, lo = _split3(adt_c)
    acs_c = _dot(tri, hi) + _dot(tri, mid) + _dot(tri, lo)
    dtr = _softplus(dtr_ref[...] + biasr_ref[...])
    adt_r = dtr * ar_ref[...]
    hi, mid, lo = _split3(adt_r)
    acs_r = _dot(hi, tri_t) + _dot(mid, tri_t) + _dot(lo, tri_t)
    tot_r = acs_r[:, 0:1] if reverse else acs_r[:, L - 1:L]
    wdt_r = jnp.exp(tot_r - acs_r) * dtr
    etot = jnp.exp(tot_r)
    ecol = jnp.exp(acs_c)

    cb16 = c_ref[...]
    bt16 = bt_ref[...]
    cbm = jnp.where(mask, _dot(cb16, bt16), 0.0)
    return dict(acs_c=acs_c, acs_r=acs_r, dtr=dtr, wdt_r=wdt_r, etot=etot, ecol=ecol, cbm=cbm,
                cf=cb16.astype(F32), btf=bt16.astype(F32))


def _ssd_head_pair(p, pre, xs_ref, dsk_ref, y_ref, h_scr):
    L = SSD_CHUNK
    lo_half = lax.broadcasted_iota(jnp.int32, (L, LANES), 1) < SSD_HEAD_DIM
    cols = slice(p * LANES, (p + 1) * LANES)
    xp = xs_ref[:, cols]
    xpb = xp.astype(BF16)
    hp = h_scr[p]
    hpb = hp.astype(BF16)
    ys, hs = [], []
    for e2 in range(2):
        e = 2 * p + e2
        col = pre["acs_c"][:, e:e + 1]
        row = pre["acs_r"][e:e + 1, :]
        ld = (pre["cbm"] * jnp.exp(jnp.minimum(col - row, 0.0)) * pre["dtr"][e:e + 1, :]).astype(BF16)
        lo_m = (pre["cf"] * pre["ecol"][:, e:e + 1]).astype(BF16)
        ls = (pre["btf"] * pre["wdt_r"][e:e + 1, :]).astype(BF16)
        ys.append(_dot(ld, xpb) + _dot(lo_m, hpb))
        hs.append(_dot(ls, xpb) + pre["etot"][e:e + 1, :] * hp)
    y_ref[:, cols] = jnp.where(lo_half, ys[0], ys[1]) + dsk_ref[:, cols] * xp
    h_scr[p] = jnp.where(lo_half, hs[0], hs[1])


def ssd_scan(xs, bc, bt, dt_cols, dt_rows, bias, a_neg, dskip, lay):
    g, tp, gw = xs.shape
    L = SSD_CHUNK
    n_chunks = tp // L
    n_valid = lay.n_valid // L
    hpg = SSD_HEADS_PER_GROUP

    def lane_pad(v):
        return jnp.pad(v.reshape(g, 1, hpg), ((0, 0), (0, 0), (0, LANES - hpg)))

    in_specs, args = [], []
    gps = SSD_GPS
    assert g % gps == 0
    for d in range(2):
        cidx = functools.partial(_chunk_index, lay=lay, reverse=(d == 1), chunk=L)
        in_specs += [pl.BlockSpec((gps, L, gw), lambda gi, k, c=cidx: (gi, c(k), 0)),
                     pl.BlockSpec((gps, L, SSD_STATE), lambda gi, k, c=cidx: (gi, c(k), 0)),
                     pl.BlockSpec((gps, L, SSD_STATE), lambda gi, k, c=cidx: (g // gps + gi, c(k), 0)),
                     pl.BlockSpec((gps, SSD_STATE, L), lambda gi, k, c=cidx: (gi, 0, c(k))),
                     pl.BlockSpec((gps, L, LANES), lambda gi, k, c=cidx: (gi, c(k), 0)),
                     pl.BlockSpec((gps, hpg, L), lambda gi, k, c=cidx: (gi, 0, c(k))),
                     pl.BlockSpec((gps, 1, LANES), lambda gi, k: (gi, 0, 0)),
                     pl.BlockSpec((gps, hpg, 1), lambda gi, k: (gi, 0, 0)),
                     pl.BlockSpec((gps, 1, LANES), lambda gi, k: (gi, 0, 0)),
                     pl.BlockSpec((gps, hpg, 1), lambda gi, k: (gi, 0, 0)),
                     pl.BlockSpec((gps, 1, gw), lambda gi, k: (gi, 0, 0))]
        args += [xs, bc, bc, bt, dt_cols[d], dt_rows[d], lane_pad(bias[d]), bias[d].reshape(g, hpg, 1),
                 lane_pad(a_neg[d]), a_neg[d].reshape(g, hpg, 1),
                 jnp.repeat(dskip[d], SSD_HEAD_DIM).reshape(g, 1, gw)]
    assert len(args) == 2 * N_SSD_IN
    out_specs = [pl.BlockSpec((gps, L, gw),
                              lambda gi, k, c=functools.partial(_chunk_index, lay=lay, reverse=(d == 1), chunk=L):
                              (gi, c(k), 0)) for d in range(2)]
    return pl.pallas_call(
        functools.partial(_ssd_scan_kernel, n_valid_chunks=n_valid),
        grid=(g // gps, n_chunks),
        in_specs=in_specs,
        out_specs=out_specs,
        out_shape=[jax.ShapeDtypeStruct((g, tp, gw), F32)] * 2,
        scratch_shapes=[pltpu.VMEM((gps, hpg // 2, SSD_STATE, LANES), F32)] * 2,
        compiler_params=_cp(("parallel", "arbitrary")),
        name="ssd_scan",
    )(*args)


def _ssd_gate_kernel(yf_ref, yb_ref, z_ref, nw_ref, a_ref):
    gw = yf_ref.shape[-1]
    for g in range(SSD_GROUPS):
        cols = slice(g * gw, (g + 1) * gw)
        gz = (yf_ref[g] + yb_ref[g]) * _silu(z_ref[:, cols])
        gn = gz * lax.rsqrt(jnp.mean(gz * gz, axis=-1, keepdims=True) + NORM_EPS)
        a_ref[:, cols] = (gn * nw_ref[:, cols]).astype(BF16)


def ssd_gate(yf, yb, z, norm_w):
    g, tp, gw = yf.shape
    d_inner = g * gw
    tm = TE
    return pl.pallas_call(
        _ssd_gate_kernel,
        grid=(tp // tm,),
        in_specs=[pl.BlockSpec((g, tm, gw), lambda i: (0, i, 0)),
                  pl.BlockSpec((g, tm, gw), lambda i: (0, i, 0)),
                  pl.BlockSpec((tm, d_inner), lambda i: (i, 0)),
                  pl.BlockSpec((1, d_inner), lambda i: (0, 0))],
        out_specs=pl.BlockSpec((tm, d_inner), lambda i: (i, 0)),
        out_shape=jax.ShapeDtypeStruct((tp, d_inner), BF16),
        compiler_params=_cp(("parallel",)),
        name="ssd_gate",
    )(yf, yb, z, norm_w.reshape(1, d_inner))


def ssd_mixer(h, lay, w_in, conv_w, conv_b, dt_bias, a_log, d_skip, norm_w):
    d_inner = SSD_GROUPS * SSD_HEADS_PER_GROUP * SSD_HEAD_DIM
    gn = SSD_GROUPS * SSD_STATE
    conv_dim = d_inner + 2 * gn
    n_heads = SSD_GROUPS * SSD_HEADS_PER_GROUP
    tp = h.shape[0]
    w16 = w_in.astype(BF16)
    w_rest = w16[:, d_inner:d_inner + conv_dim]
    z = mm(h, w16[:, :d_inner], tn=512, name="ssd_z")
    xs = mm_conv(h, w_rest, conv_w, conv_b, lay, tn=SSD_HEADS_PER_GROUP * SSD_HEAD_DIM, out_dtype=F32,
                 act="silu", gw=SSD_HEADS_PER_GROUP * SSD_HEAD_DIM, col0=0, ncols=d_inner, name="ssd_x")
    bc = mm_conv(h, w_rest, conv_w, conv_b, lay, tn=512, out_dtype=BF16, act="silu",
                 gw=SSD_STATE, col0=d_inner, ncols=2 * gn, name="ssd_bc")
    dt_pre = mm(h, w16[:, d_inner + conv_dim:], name="ssd_dt")
    bt = jnp.transpose(bc[:SSD_GROUPS], (0, 2, 1))
    a_neg = -jnp.exp(a_log.astype(F32))
    dt_cols, dt_rows = [], []
    for d in range(2):
        dt_d = dt_pre[:, d * n_heads:(d + 1) * n_heads].reshape(tp, SSD_GROUPS, SSD_HEADS_PER_GROUP)
        dt_cols.append(jnp.pad(jnp.transpose(dt_d, (1, 0, 2)),
                               ((0, 0), (0, 0), (0, LANES - SSD_HEADS_PER_GROUP))))
        dt_rows.append(jnp.transpose(dt_d, (1, 2, 0)))
    y_fwd, y_rev = ssd_scan(xs, bc, bt, dt_cols, dt_rows, dt_bias, a_neg, d_skip, lay)
    return ssd_gate(y_fwd, y_rev, z, norm_w)


LRU_TB = 128


def _lru_scan_kernel(u_ref, gw_ref, gb_ref, lam_ref, o_ref, carry, *, reverse, n_valid_chunks):
    k = pl.program_id(0)
    tb = LRU_TB
    bw = u_ref.shape[1] // LRU_BLOCKS

    @pl.when(k == 0)
    def _():
        carry[...] = jnp.zeros_like(carry)

    @pl.when(k >= n_valid_chunks)
    def _():
        o_ref[...] = jnp.zeros_like(o_ref)

    @pl.when(k < n_valid_chunks)
    def _():
        u = u_ref[...]
        ub = u.astype(BF16)

        def pre(z):
            parts = [_dot(ub[:, n * bw:(n + 1) * bw], gw_ref[z, n]) for n in range(LRU_BLOCKS)]
            return jnp.concatenate(parts, axis=1) + gb_ref[z]

        r = jax.nn.sigmoid(pre(0))
        gi = jax.nn.sigmoid(pre(1))
        log_a = (-LRU_C) * r * _softplus(-lam_ref[...])
        a = jnp.exp(log_a)
        th = jnp.tanh(log_a)
        b = jnp.sqrt(-2.0 * th / (1.0 - th)) * (gi * u)
        n_g = tb // SUBLANES
        w = a.shape[1]
        sub = lax.broadcasted_iota(jnp.int32, (tb, 1), 0) & (SUBLANES - 1)

        def roll_in_group(x, shift):
            return pltpu.roll(x.reshape(n_g, SUBLANES, w), shift, 1).reshape(tb, w)

        sh = 1
        while sh < SUBLANES:
            if reverse:
                keep = sub < SUBLANES - sh
                a_s = jnp.where(keep, roll_in_group(a, SUBLANES - sh), 1.0)
                b_s = jnp.where(keep, roll_in_group(b, SUBLANES - sh), 0.0)
            else:
                keep = sub >= sh
                a_s = jnp.where(keep, roll_in_group(a, sh), 1.0)
                b_s = jnp.where(keep, roll_in_group(b, sh), 0.0)
            b = a * b_s + b
            a = a * a_s
            sh *= 2
        c = carry[0:1, :]
        for gi in (range(n_g - 1, -1, -1) if reverse else range(n_g)):
            rows = slice(gi * SUBLANES, (gi + 1) * SUBLANES)
            hg = b[rows, :] + a[rows, :] * c
            o_ref[rows, :] = hg
            c = hg[0:1, :] if reverse else hg[SUBLANES - 1:SUBLANES, :]
        carry[0:1, :] = c


def lru_scan(u, gate_w, gate_b, lam, lay, reverse):
    tp, w = u.shape
    tb = LRU_TB
    bw = w // LRU_BLOCKS
    cidx = functools.partial(_chunk_index, lay=lay, reverse=reverse, chunk=tb)
    return pl.pallas_call(
        functools.partial(_lru_scan_kernel, reverse=reverse, n_valid_chunks=lay.n_valid // tb),
        grid=(tp // tb,),
        in_specs=[pl.BlockSpec((tb, w), lambda k: (cidx(k), 0)),
                  pl.BlockSpec((2, LRU_BLOCKS, bw, bw), lambda k: (0, 0, 0, 0)),
                  pl.BlockSpec((2, 1, w), lambda k: (0, 0, 0)),
                  pl.BlockSpec((1, w), lambda k: (0, 0))],
        out_specs=pl.BlockSpec((tb, w), lambda k: (cidx(k), 0)),
        out_shape=jax.ShapeDtypeStruct((tp, w), F32),
        scratch_shapes=[pltpu.VMEM((8, w), F32)],
        compiler_params=_cp(("arbitrary",)),
        name="lru_scan_rev" if reverse else "lru_scan_fwd",
    )(u, gate_w.astype(BF16), gate_b.reshape(2, 1, w), lam.reshape(1, w))


def _lru_gate_kernel(hf_ref, hb_ref, gp_ref, a_ref):
    a_ref[...] = ((hf_ref[...] + hb_ref[...]) * jax.nn.gelu(gp_ref[...], approximate=True)).astype(BF16)


def lru_gate(hf, hb, gate_pre):
    tp, w = hf.shape
    tm = TE
    spec = pl.BlockSpec((tm, w), lambda i: (i, 0))
    return pl.pallas_call(
        _lru_gate_kernel,
        grid=(tp // tm,),
        in_specs=[spec, spec, spec],
        out_specs=spec,
        out_shape=jax.ShapeDtypeStruct((tp, w), BF16),
        compiler_params=_cp(("parallel",)),
        name="lru_gate",
    )(hf, hb, gate_pre)


def lru_mixer(h, lay, w_in, conv_w, conv_b, gate_w, gate_b, lam):
    w = conv_w.shape[1]
    w16 = w_in.astype(BF16)
    gate_pre = mm(h, w16[:, :w], tn=512, name="lru_gate_proj")
    u = mm_conv(h, w16[:, w:], conv_w, conv_b, lay, tn=512, out_dtype=F32, name="lru_x")
    hs = [lru_scan(u, gate_w[d], gate_b[d], lam[d], lay, reverse=(d == 1)) for d in range(2)]
    return lru_gate(hs[0], hs[1], gate_pre)


def rope_tables(lay):
    n = lay.n_lat
    row = jnp.repeat(jnp.arange(n // GRID_W, dtype=F32), GRID_W)
    col = jnp.tile(jnp.arange(GRID_W, dtype=F32), n // GRID_W)
    n_freq = ROPE_DIM // 4
    inv = ROPE_THETA ** (-jnp.arange(n_freq, dtype=F32) / n_freq)
    ang = jnp.concatenate([row[:, None] * inv, col[:, None] * inv], axis=-1)
    cos, sin = jnp.cos(ang), jnp.sin(ang)
    zero = jnp.zeros_like(sin)
    c = jnp.tile(cos, (1, 4))
    s1 = jnp.tile(jnp.concatenate([-sin, zero], axis=1), (1, 2))
    s2 = jnp.tile(jnp.concatenate([zero, sin], axis=1), (1, 2))
    pad = lay.tp - n
    c = jnp.concatenate([c, jnp.ones((pad, LANES), F32)], axis=0)
    s1 = jnp.pad(s1, ((0, pad), (0, 0)))
    s2 = jnp.pad(s2, ((0, pad), (0, 0)))
    return c, s1, s2


MLA_QK_PAD = 256
MLA_HPS = 2


def _mla_q_kernel(a_ref, g_ref, w_ref, c_ref, s1_ref, s2_ref, qr_ref, qp_ref, nscr, *, scale):
    @pl.when(pl.program_id(1) == 0)
    def _():
        x = a_ref[...]
        nscr[...] = (x * lax.rsqrt(jnp.mean(x * x, axis=-1, keepdims=True) + NORM_EPS) * g_ref[...]).astype(BF16)

    acc = _dot(nscr[...], w_ref[...]) * scale
    c, s1, s2 = c_ref[...], s1_ref[...], s2_ref[...]
    for hh in range(MLA_HPS):
        a_h = acc[:, hh * MLA_QK_PAD:(hh + 1) * MLA_QK_PAD]
        qp_ref[hh] = a_h.astype(BF16)
        rot = _rope128(a_h[:, LANES:], c, s1, s2)
        qr_ref[hh] = jnp.concatenate([a_h[:, :LANES], rot], axis=1).astype(BF16)


def _mla_kv_kernel(a_ref, kr_ref, g_ref, w_ref, c_ref, s1_ref, s2_ref, k_ref, v_ref, nscr, krscr):
    @pl.when(pl.program_id(1) == 0)
    def _():
        x = a_ref[...]
        nscr[...] = (x * lax.rsqrt(jnp.mean(x * x, axis=-1, keepdims=True) + NORM_EPS) * g_ref[...]).astype(BF16)
        krscr[...] = _rope128(kr_ref[...], c_ref[...], s1_ref[...], s2_ref[...]).astype(BF16)

    acc = _dot(nscr[...], w_ref[...])
    ones = jnp.ones((acc.shape[0], LANES), BF16)
    for hh in range(MLA_HPS):
        a_h = acc[:, hh * (MLA_NOPE + MLA_V):(hh + 1) * (MLA_NOPE + MLA_V)]
        k_ref[hh] = jnp.concatenate([a_h[:, :MLA_NOPE].astype(BF16), krscr[...]], axis=1)
        v_ref[hh] = jnp.concatenate([a_h[:, MLA_NOPE:].astype(BF16), ones], axis=1)


MLA_TQ = 1024
MLA_TK = 512
MLA_SUB = 256


def _mla_ctx_attn_kernel(qp_ref, k_ref, v_ref, o_ref):
    s = _dot_nt(qp_ref[...], k_ref[...])
    p = jnp.exp2(s - jnp.max(s, axis=-1, keepdims=True)).astype(BF16)
    acc = _dot(p, v_ref[...])
    o_ref[...] = (acc[:, :MLA_V] / acc[:, MLA_V:]).astype(o_ref.dtype)


def _mla_attn_kernel(qr_ref, qp_ref, k_ref, v_ref, o_ref, acc_ref, s_ref, *, n_lat, n_ctx):
    tq, tk, sub = MLA_TQ, MLA_TK, MLA_SUB
    nsub = tq // sub
    n_kt = n_lat // tk
    assert n_kt % 2 == 0
    rows = [slice(c * sub, (c + 1) * sub) for c in range(nsub)]

    def qk_lat(c, tile):
        start = pl.multiple_of(tile * tk, tk)
        return _dot_nt(qr_ref[rows[c], :], k_ref[pl.ds(start, tk), :])

    def consume(c, s, vv, m):
        m_new = jnp.maximum(m, jnp.max(s, axis=-1, keepdims=True))
        alpha = jnp.exp2(m - m_new)
        p = jnp.exp2(s - m_new).astype(BF16)
        acc_ref[rows[c], :] = alpha * acc_ref[rows[c], :] + _dot(p, vv)
        return m_new

    def v_tile(tile):
        return v_ref[pl.ds(pl.multiple_of(tile * tk, tk), tk), :]

    def half(ms, slot, tile):
        vv = v_tile(tile)
        out = []
        for c in range(nsub):
            s = s_ref[slot, rows[c], :]
            s_ref[1 - slot, rows[c], :] = qk_lat(c, tile + 1)
            out.append(consume(c, s, vv, ms[c]))
        return tuple(out)

    def finish():
        acc = acc_ref[...]
        o_ref[...] = (acc[:, :MLA_V] / acc[:, MLA_V:]).astype(o_ref.dtype)

    acc_ref[...] = jnp.zeros_like(acc_ref)
    m0 = tuple(jnp.full((sub, 1), -jnp.inf, F32) for _ in range(nsub))
    k_ctx = lambda: k_ref[pl.ds(n_lat, n_ctx), :]
    v_ctx = lambda: v_ref[pl.ds(n_lat, n_ctx), :]

    for c in range(nsub):
        s_ref[0, rows[c], :] = qk_lat(c, 0)

    def pair(pi, ms):
        ms = half(ms, 0, 2 * pi)
        return half(ms, 1, 2 * pi + 1)

    ms = lax.fori_loop(0, n_kt // 2 - 1, pair, m0)
    ms = half(ms, 0, n_kt - 2)
    vv = v_tile(n_kt - 1)
    for c in range(nsub):
        s = s_ref[1, rows[c], :]
        s_c = _dot_nt(qp_ref[rows[c], :], k_ctx())
        m = consume(c, s, vv, ms[c])
        consume(c, s_c, v_ctx(), m)
    finish()


def mla_mixer(h, lay, tables, w_in, q_norm, kv_norm, w_q_up, w_kv_up):
    tp = h.shape[0]
    d = h.shape[1]
    c, s1, s2 = tables
    qk = MLA_NOPE + ROPE_DIM
    w_lat = jnp.concatenate([
        w_in[:, :MLA_Q_RANK], jnp.zeros((d, 256), F32),
        w_in[:, MLA_Q_RANK:MLA_Q_RANK + MLA_KV_RANK],
        w_in[:, MLA_Q_RANK + MLA_KV_RANK:], jnp.zeros((d, LANES - ROPE_DIM), F32)], axis=1).astype(BF16)
    lat = mm(h, w_lat, name="mla_lat")
    wq = jnp.pad(w_q_up.reshape(MLA_Q_RANK, MLA_HEADS, qk),
                 ((0, 0), (0, 0), (0, MLA_QK_PAD - qk))).reshape(MLA_Q_RANK, MLA_HEADS * MLA_QK_PAD).astype(BF16)
    n_t = tp // TM
    tab_spec = pl.BlockSpec((TM, LANES), lambda i, j: (i, 0))
    q_rot, q_plain = pl.pallas_call(
        functools.partial(_mla_q_kernel, scale=qk ** -0.5 * LOG2E),
        grid=(n_t, MLA_HEADS // MLA_HPS),
        in_specs=[pl.BlockSpec((TM, MLA_Q_RANK), lambda i, j: (i, 0)),
                  pl.BlockSpec((1, MLA_Q_RANK), lambda i, j: (0, 0)),
                  pl.BlockSpec((MLA_Q_RANK, MLA_HPS * MLA_QK_PAD), lambda i, j: (0, j)),
                  tab_spec, tab_spec, tab_spec],
        out_specs=[pl.BlockSpec((MLA_HPS, TM, MLA_QK_PAD), lambda i, j: (j, i, 0)),
                   pl.BlockSpec((MLA_HPS, TM, MLA_QK_PAD), lambda i, j: (j, i, 0))],
        out_shape=[jax.ShapeDtypeStruct((MLA_HEADS, tp, MLA_QK_PAD), BF16)] * 2,
        scratch_shapes=[pltpu.VMEM((TM, MLA_Q_RANK), BF16)],
        compiler_params=_cp(("parallel", "arbitrary")),
        name="mla_q",
    )(lat, q_norm.reshape(1, -1), wq, c, s1, s2)
    kv_col = (MLA_Q_RANK + 256) // MLA_KV_RANK
    kr_col = (MLA_Q_RANK + 256 + MLA_KV_RANK) // LANES
    k_all, v_all = pl.pallas_call(
        _mla_kv_kernel,
        grid=(n_t, MLA_HEADS // MLA_HPS),
        in_specs=[pl.BlockSpec((TM, MLA_KV_RANK), lambda i, j: (i, kv_col)),
                  pl.BlockSpec((TM, LANES), lambda i, j: (i, kr_col)),
                  pl.BlockSpec((1, MLA_KV_RANK), lambda i, j: (0, 0)),
                  pl.BlockSpec((MLA_KV_RANK, MLA_HPS * (MLA_NOPE + MLA_V)), lambda i, j: (0, j)),
                  tab_spec, tab_spec, tab_spec],
        out_specs=[pl.BlockSpec((MLA_HPS, TM, MLA_QK_PAD), lambda i, j: (j, i, 0)),
                   pl.BlockSpec((MLA_HPS, TM, MLA_V + LANES), lambda i, j: (j, i, 0))],
        out_shape=[jax.ShapeDtypeStruct((MLA_HEADS, tp, MLA_QK_PAD), BF16),
                   jax.ShapeDtypeStruct((MLA_HEADS, tp, MLA_V + LANES), BF16)],
        scratch_shapes=[pltpu.VMEM((TM, MLA_KV_RANK), BF16), pltpu.VMEM((TM, LANES), BF16)],
        compiler_params=_cp(("parallel", "arbitrary")),
        name="mla_kv",
    )(lat, lat, kv_norm.reshape(1, -1), w_kv_up.astype(BF16), c, s1, s2)
    tq = min(MLA_TQ, lay.n_lat)
    assert lay.n_lat % tq == 0 and tq % MLA_SUB == 0
    o_lat = pl.pallas_call(
        functools.partial(_mla_attn_kernel, n_lat=lay.n_lat, n_ctx=lay.n_ctx),
        grid=(MLA_HEADS, lay.n_lat // tq),
        in_specs=[pl.BlockSpec((None, tq, MLA_QK_PAD), lambda hh, i: (hh, i, 0)),
                  pl.BlockSpec((None, tq, MLA_QK_PAD), lambda hh, i: (hh, i, 0)),
                  pl.BlockSpec((None, tp, MLA_QK_PAD), lambda hh, i: (hh, 0, 0)),
                  pl.BlockSpec((None, tp, MLA_V + LANES), lambda hh, i: (hh, 0, 0))],
        out_specs=pl.BlockSpec((tq, MLA_V), lambda hh, i: (i, hh)),
        out_shape=jax.ShapeDtypeStruct((lay.n_lat, MLA_HEADS * MLA_V), BF16),
        scratch_shapes=[pltpu.VMEM((tq, MLA_V + LANES), F32), pltpu.VMEM((2, tq, MLA_TK), F32)],
        compiler_params=_cp(("parallel", "parallel")),
        name="mla_attn",
    )(q_rot, q_plain, k_all, v_all)
    ctx_rows = tp - lay.n_lat
    ctx_blk = lay.n_lat // lay.n_ctx
    o_ctx = pl.pallas_call(
        _mla_ctx_attn_kernel,
        grid=(MLA_HEADS,),
        in_specs=[pl.BlockSpec((None, ctx_rows, MLA_QK_PAD), lambda hh: (hh, lay.n_lat // ctx_rows, 0)),
                  pl.BlockSpec((None, lay.n_ctx, MLA_QK_PAD), lambda hh: (hh, ctx_blk, 0)),
                  pl.BlockSpec((None, lay.n_ctx, MLA_V + LANES), lambda hh: (hh, ctx_blk, 0))],
        out_specs=pl.BlockSpec((ctx_rows, MLA_V), lambda hh: (0, hh)),
        out_shape=jax.ShapeDtypeStruct((ctx_rows, MLA_HEADS * MLA_V), BF16),
        compiler_params=_cp(("parallel",)),
        name="mla_attn_ctx",
    )(q_plain, k_all, v_all)
    return jnp.concatenate([o_lat, o_ctx], axis=0)


def _mm_bias_rope_kernel(*refs, scale, rope, emit_plain):
    a_ref, w_ref, b_ref = refs[:3]
    if rope:
        c_ref, s1_ref, s2_ref = refs[3:6]
        outs = refs[6:]
    else:
        outs = refs[3:]
    acc = (_dot(a_ref[...], w_ref[...]) + b_ref[...]) * scale
    if not rope:
        outs[0][...] = acc.astype(outs[0].dtype)
        return
    c, s1, s2 = c_ref[...], s1_ref[...], s2_ref[...]
    tn = acc.shape[1]
    for s in range(tn // LANES):
        cols = slice(s * LANES, (s + 1) * LANES)
        outs[0][:, cols] = _rope128(acc[:, cols], c, s1, s2).astype(outs[0].dtype)
    if emit_plain:
        outs[1][...] = acc.astype(outs[1].dtype)


def mm_bias_rope(a, w, b, tables, *, scale=1.0, rope=True, emit_plain=False, name="mm_bias_rope"):
    m, k = a.shape
    n = w.shape[1]
    tn = 512
    tm = m // RESID_ROW_TILES
    assert m % RESID_ROW_TILES == 0 and tm % BF16_ROWS == 0
    in_specs = [pl.BlockSpec((tm, k), lambda i, j: (i, 0)),
                pl.BlockSpec((k, tn), lambda i, j: (0, j)),
                pl.BlockSpec((1, tn), lambda i, j: (0, j))]
    args = [a, w, b.reshape(1, n)]
    if rope:
        in_specs += [pl.BlockSpec((tm, LANES), lambda i, j: (i, 0))] * 3
        args += list(tables)
    n_out = 2 if (rope and emit_plain) else 1
    out_spec = pl.BlockSpec((tm, tn), lambda i, j: (i, j))
    res = pl.pallas_call(
        functools.partial(_mm_bias_rope_kernel, scale=scale, rope=rope, emit_plain=emit_plain),
        grid=(RESID_ROW_TILES, n // tn),
        in_specs=in_specs,
        out_specs=[out_spec] * n_out,
        out_shape=[jax.ShapeDtypeStruct((m, n), BF16)] * n_out,
        compiler_params=_cp(("parallel", "parallel")),
        name=name,
    )(*args)
    return res if n_out == 2 else res[0]


SWA_BPS = 4


def _swa_attn_kernel(qr_ref, qp_ref, kp_ref, kc_ref, kn_ref, vp_ref, vc_ref, vn_ref,
                     kctx_ref, vctx_ref, sink_ref, o_ref, *, n_lat):
    b = pl.program_id(1)
    blk = ATTN_BLOCK
    nb = n_lat // blk

    @pl.when(b * SWA_BPS >= nb)
    def _():
        o_ref[...] = jnp.zeros_like(o_ref)

    @pl.when(b * SWA_BPS < nb)
    def _():
        lane = lax.broadcasted_iota(jnp.int32, (blk, LANES), 1)
        r = lax.broadcasted_iota(jnp.int32, (blk, blk), 0)
        c = lax.broadcasted_iota(jnp.int32, (blk, blk), 1)
        lo_half = lane < SWA_HEAD_DIM
        kwin = jnp.concatenate([kp_ref[...], kc_ref[...], kn_ref[...]], axis=0)
        vwin = jnp.concatenate([vp_ref[...], vc_ref[...], vn_ref[...]], axis=0)
        ones_w = jnp.ones(vwin.shape, BF16)
        vwe = jnp.concatenate([vwin, ones_w], axis=1)
        vce = jnp.concatenate([vctx_ref[...], ones_w[:vctx_ref.shape[0]]], axis=1)
        kctx = kctx_ref[...]
        n_c = kctx.shape[0]
        scores = []
        for t in range(SWA_BPS):
            gb = b * SWA_BPS + t
            rows = slice(t * blk, (t + 1) * blk)
            win = slice(t * blk, (t + 3) * blk)
            bias_p = jnp.where((c >= r) & (gb > 0), 0.0, -jnp.inf)
            bias_n = jnp.where((c <= r) & (gb < nb - 1), 0.0, -jnp.inf)
            for g in range(SWA_GROUP):
                cols = slice((g // 2) * LANES, (g // 2 + 1) * LANES)
                keep = lo_half if g % 2 == 0 else jnp.logical_not(lo_half)
                qr = qr_ref[rows, cols]
                qp = qp_ref[rows, cols]
                qr = jnp.where(keep, qr, jnp.zeros_like(qr))
                qp = jnp.where(keep, qp, jnp.zeros_like(qp))
                s_c = _dot_nt(qp, kctx)
                s_w = _dot_nt(qr, kwin[win])
                scores.append(jnp.concatenate([s_c, s_w[:, :blk] + bias_p, s_w[:, blk:2 * blk],
                                               s_w[:, 2 * blk:] + bias_n], axis=1))
        for t in range(SWA_BPS):
            rows = slice(t * blk, (t + 1) * blk)
            win = slice(t * blk, (t + 3) * blk)
            outs = []
            for g in range(SWA_GROUP):
                s = scores[t * SWA_GROUP + g]
                sk = sink_ref[g:g + 1, 0:1] * LOG2E
                m = jnp.maximum(jnp.max(s, axis=-1, keepdims=True), sk)
                p = jnp.exp2(s - m).astype(BF16)
                oe = _dot(p[:, :n_c], vce) + _dot(p[:, n_c:], vwe[win])
                outs.append(oe[:, :LANES] / (oe[:, LANES:] + jnp.exp2(sk - m)))
            o_ref[rows, :] = jnp.concatenate(
                [jnp.where(lo_half, outs[0], outs[1]), jnp.where(lo_half, outs[2], outs[3])],
                axis=1).astype(o_ref.dtype)


def swa_mixer(h, lay, tables, w_qkv, b_qkv, sink):
    tp = h.shape[0]
    d = h.shape[1]
    hd = SWA_HEAD_DIM
    qd = SWA_Q_HEADS * hd
    kd = SWA_KV_HEADS * hd

    def dup_heads(wm):
        lead = wm.shape[:-1]
        w3 = wm.reshape(lead + (SWA_KV_HEADS, 1, hd))
        return jnp.broadcast_to(w3, lead + (SWA_KV_HEADS, 2, hd)).reshape(lead + (2 * kd,))

    w16 = w_qkv.astype(BF16)
    q_rot, q_plain = mm_bias_rope(h, w16[:, :qd], b_qkv[:qd], tables, scale=hd ** -0.5 * LOG2E,
                                  emit_plain=True, name="swa_q")
    k_dup = mm_bias_rope(h, dup_heads(w16[:, qd:qd + kd]), dup_heads(b_qkv[qd:qd + kd]), tables, name="swa_k")
    v_dup = mm_bias_rope(h, dup_heads(w16[:, qd + kd:]), dup_heads(b_qkv[qd + kd:]), tables,
                         rope=False, name="swa_v")
    blk = ATTN_BLOCK
    nb = lay.n_lat // blk
    ctx_blk = lay.n_lat // lay.n_ctx
    sink_t = jnp.broadcast_to(jnp.pad(sink.astype(F32).reshape(SWA_KV_HEADS, SWA_GROUP),
                                      ((0, 0), (0, 8 - SWA_GROUP)))[:, :, None], (SWA_KV_HEADS, 8, LANES))
    qw = SWA_GROUP * hd
    bps = SWA_BPS
    assert nb % bps == 0 and tp % (blk * bps) == 0
    q_spec = pl.BlockSpec((blk * bps, qw), lambda hh, b: (b, hh))
    prev_spec = pl.BlockSpec((blk, LANES), lambda hh, b: (jnp.clip(b * bps - 1, 0, nb - 1), hh))
    cur_spec = pl.BlockSpec((blk * bps, LANES), lambda hh, b: (jnp.minimum(b, nb // bps - 1), hh))
    next_spec = pl.BlockSpec((blk, LANES), lambda hh, b: (jnp.clip(b * bps + bps, 0, nb - 1), hh))
    ctx_spec = pl.BlockSpec((lay.n_ctx, LANES), lambda hh, b: (ctx_blk, hh))
    return pl.pallas_call(
        functools.partial(_swa_attn_kernel, n_lat=lay.n_lat),
        grid=(SWA_KV_HEADS, tp // (blk * bps)),
        in_specs=[q_spec, q_spec, prev_spec, cur_spec, next_spec,
                  prev_spec, cur_spec, next_spec, ctx_spec, ctx_spec,
                  pl.BlockSpec((None, 8, LANES), lambda hh, b: (hh, 0, 0))],
        out_specs=pl.BlockSpec((blk * bps, qw), lambda hh, b: (b, hh)),
        out_shape=jax.ShapeDtypeStruct((tp, qd), BF16),
        compiler_params=_cp(("parallel", "parallel")),
        name="swa_attn",
    )(q_rot, q_plain, k_dup, k_dup, k_dup, v_dup, v_dup, v_dup, k_dup, v_dup, sink_t)


def conv_ffn_hidden(h, lay, w_up, conv_w, conv_b):
    hidden = w_up.shape[1] // 2
    return mm_conv(h, w_up.astype(BF16), conv_w, conv_b, lay, tn=512, out_dtype=BF16, gated=True,
                   ncols=hidden, name="ffn_up")


def kernel(x, c, ctx, c_ctx, ada_w, ada_b, norm_mix, norm_ffn, ffn_up, ffn_conv_w, ffn_conv_b, ffn_down, final_norm, ssd_in, ssd_conv_w, ssd_conv_b, ssd_dt_bias, ssd_a_log, ssd_d, ssd_norm, ssd_out, lru_in, lru_conv_w, lru_conv_b, lru_gate_w, lru_gate_b, lru_lambda, lru_out, mla_in, mla_q_norm, mla_kv_norm, mla_q_up, mla_kv_up, mla_out, swa_qkv, swa_qkv_b, swa_sink, swa_out):
    bsz, n_lat, d = x.shape
    n_ctx = ctx.shape[1]
    depth = ada_w.shape[0]
    assert bsz == 1 and depth == 4, "one sample, four layers (one of each mixer)"
    lay = make_layout(n_lat, n_ctx)
    xs = jnp.concatenate([x[0], ctx[0], jnp.zeros((lay.tp - lay.n_valid, d), x.dtype)], axis=0)
    cv = jnp.concatenate([c, c_ctx[None, :], jnp.zeros((6, d), F32)], axis=0)
    mods = adaln(cv, ada_w, ada_b)
    tables = rope_tables(lay)
    for i in range(depth):
        mod = jnp.pad(mods[i, :2].reshape(2, ADA_CHUNKS, d), ((0, 0), (0, 8 - ADA_CHUNKS), (0, 0)))
        h = norm_mod(xs, norm_mix[i], mod, 0, 1, lay)
        if i == 0:
            a = ssd_mixer(h, lay, ssd_in[0], ssd_conv_w[0], ssd_conv_b[0], ssd_dt_bias[0], ssd_a_log[0],
                          ssd_d[0], ssd_norm[0])
            w_o = ssd_out[0]
        elif i == 1:
            a = lru_mixer(h, lay, lru_in[0], lru_conv_w[0], lru_conv_b[0], lru_gate_w[0], lru_gate_b[0],
                          lru_lambda[0])
            w_o = lru_out[0]
        elif i == 2:
            a = mla_mixer(h, lay, tables, mla_in[0], mla_q_norm[0], mla_kv_norm[0], mla_q_up[0], mla_kv_up[0])
            w_o = mla_out[0]
        else:
            a = swa_mixer(h, lay, tables, swa_qkv[0], swa_qkv_b[0], swa_sink[0])
            w_o = swa_out[0]
        xs = mm_resid(a, w_o.astype(BF16), xs, mod, 2, lay, name="mix_out")
        h = norm_mod(xs, norm_ffn[i], mod, 3, 4, lay)
        hid = conv_ffn_hidden(h, lay, ffn_up[i], ffn_conv_w[i], ffn_conv_b[i])
        xs = mm_resid(hid, ffn_down[i].astype(BF16), xs, mod, 5, lay, name="ffn_down")
    return rmsnorm_rows(xs, final_norm, n_lat)[None]
```

```python
import functools
import math
from typing import NamedTuple

import jax
import jax.numpy as jnp
from jax import lax
from jax.experimental import pallas as pl
from jax.experimental.pallas import tpu as pltpu

F32 = jnp.float32
BF16 = jnp.bfloat16

GRID_W = 64
NORM_EPS = 1e-6
ADA_CHUNKS = 6
ROPE_THETA = 10000.0
ROPE_DIM = 64
SSD_HEAD_DIM = 64
SSD_GROUPS = 8
SSD_HEADS_PER_GROUP = 8
SSD_STATE = 128
SSD_CHUNK = 128
LRU_BLOCKS = 8
LRU_C = 8.0
MLA_HEADS = 16
MLA_Q_RANK = 768
MLA_KV_RANK = 512
MLA_NOPE = 128
MLA_V = 128
SWA_Q_HEADS = 32
SWA_KV_HEADS = 8
SWA_GROUP = 4
SWA_HEAD_DIM = 64
SWA_WINDOW = 128
ATTN_BLOCK = 128

LANES = 128
SUBLANES = 8
MXU_COLS = 256
BF16_ROWS = 16
TM = 512
TE = 256
RESID_ROW_TILES = 8
VMEM_LIMIT = 56 * 2 ** 20
LOG2E = math.log2(math.e)


class Layout(NamedTuple):
    n_lat: int
    n_ctx: int
    tp: int

    @property
    def n_valid(self):
        return self.n_lat + self.n_ctx


def make_layout(n_lat, n_ctx):
    assert n_lat % TM == 0 and n_ctx % SSD_CHUNK == 0
    ctx_pad = -(-n_ctx // TM) * TM
    return Layout(n_lat, n_ctx, n_lat + ctx_pad)


def _cp(sem):
    return pltpu.CompilerParams(dimension_semantics=sem, vmem_limit_bytes=VMEM_LIMIT)


def _silu(x):
    return x * jax.nn.sigmoid(x)


def _softplus(x):
    return jnp.maximum(x, 0.0) + jnp.log1p(jnp.exp(-jnp.abs(x)))


def _dot(a, b):
    return jnp.dot(a, b, preferred_element_type=F32)


def _dot_nt(a, b):
    return lax.dot_general(a, b, (((1,), (1,)), ((), ())), preferred_element_type=F32)


def _split3(x):
    hi = x.astype(BF16)
    r = x - hi.astype(F32)
    mid = r.astype(BF16)
    lo = (r - mid.astype(F32)).astype(BF16)
    return hi, mid, lo


def _rope128(v, c, s1, s2):
    return v * c + pltpu.roll(v, 96, 1) * s1 + pltpu.roll(v, 32, 1) * s2


def _adaln_kernel(cv_ref, w_ref, b_ref, o_ref):
    s = _silu(cv_ref[...]).astype(BF16)
    o_ref[...] = _dot(s, w_ref[...].astype(BF16)) + b_ref[...]


def adaln(cv, ada_w, ada_b):
    depth, d, n = ada_w.shape
    tn = 1024
    return pl.pallas_call(
        _adaln_kernel,
        grid=(depth, n // tn),
        in_specs=[pl.BlockSpec((8, d), lambda l, j: (0, 0)),
                  pl.BlockSpec((None, d, tn), lambda l, j: (l, 0, j)),
                  pl.BlockSpec((None, 1, tn), lambda l, j: (l, 0, j))],
        out_specs=pl.BlockSpec((None, 8, tn), lambda l, j: (l, 0, j)),
        out_shape=jax.ShapeDtypeStruct((depth, 8, n), F32),
        compiler_params=_cp(("parallel", "parallel")),
        name="adaln",
    )(cv, ada_w, ada_b.reshape(depth, 1, n))


def _norm_mod_kernel(x_ref, g_ref, mod_ref, h_ref, *, shift_idx, scale_idx, n_valid, tm):
    x = x_ref[...]
    y = x * lax.rsqrt(jnp.mean(x * x, axis=-1, keepdims=True) + NORM_EPS) * g_ref[...]
    h = y * (1.0 + mod_ref[scale_idx:scale_idx + 1, :]) + mod_ref[shift_idx:shift_idx + 1, :]
    row = pl.program_id(0) * tm + lax.broadcasted_iota(jnp.int32, (tm, 1), 0)
    h_ref[...] = jnp.where(row < n_valid, h, 0.0).astype(BF16)


def norm_mod(x, g, mod, shift_idx, scale_idx, lay):
    tp, d = x.shape
    tm = TM
    return pl.pallas_call(
        functools.partial(_norm_mod_kernel, shift_idx=shift_idx, scale_idx=scale_idx,
                          n_valid=lay.n_valid, tm=tm),
        grid=(tp // tm,),
        in_specs=[pl.BlockSpec((tm, d), lambda i: (i, 0)),
                  pl.BlockSpec((1, d), lambda i: (0, 0)),
                  pl.BlockSpec((None, 8, d), lambda i: ((i * tm >= lay.n_lat).astype(jnp.int32), 0, 0))],
        out_specs=pl.BlockSpec((tm, d), lambda i: (i, 0)),
        out_shape=jax.ShapeDtypeStruct((tp, d), BF16),
        compiler_params=_cp(("parallel",)),
        name="norm_mod",
    )(x, g.reshape(1, d), mod)


def _rmsnorm_kernel(x_ref, g_ref, o_ref):
    x = x_ref[...]
    o_ref[...] = x * lax.rsqrt(jnp.mean(x * x, axis=-1, keepdims=True) + NORM_EPS) * g_ref[...]


def rmsnorm_rows(x, g, rows):
    d = x.shape[1]
    tm = TM
    return pl.pallas_call(
        _rmsnorm_kernel,
        grid=(rows // tm,),
        in_specs=[pl.BlockSpec((tm, d), lambda i: (i, 0)), pl.BlockSpec((1, d), lambda i: (0, 0))],
        out_specs=pl.BlockSpec((tm, d), lambda i: (i, 0)),
        out_shape=jax.ShapeDtypeStruct((rows, d), F32),
        compiler_params=_cp(("parallel",)),
        name="final_norm",
    )(x, g.reshape(1, d))


def _mm_kernel(a_ref, w_ref, o_ref):
    o_ref[...] = _dot(a_ref[...], w_ref[...]).astype(o_ref.dtype)


def mm(a, w, out_dtype=F32, tn=None, name="mm"):
    m, k = a.shape
    n = w.shape[1]
    tn = n if tn is None else tn
    tm = m // RESID_ROW_TILES
    assert m % RESID_ROW_TILES == 0 and tm % BF16_ROWS == 0
    return pl.pallas_call(
        _mm_kernel,
        grid=(RESID_ROW_TILES, n // tn),
        in_specs=[pl.BlockSpec((tm, k), lambda i, j: (i, 0)),
                  pl.BlockSpec((k, tn), lambda i, j: (0, j))],
        out_specs=pl.BlockSpec((tm, tn), lambda i, j: (i, j)),
        out_shape=jax.ShapeDtypeStruct((m, n), out_dtype),
        compiler_params=_cp(("parallel", "parallel")),
        name=name,
    )(a, w)


def _mm_resid_kernel(a_ref, w_ref, x_ref, mod_ref, o_ref, *, gate_idx, n_lat, tm):
    acc = _dot(a_ref[...], w_ref[...])
    row = pl.program_id(0) * tm + lax.broadcasted_iota(jnp.int32, (tm, 1), 0)
    gate = jnp.where(row < n_lat, mod_ref[0, gate_idx:gate_idx + 1, :], mod_ref[1, gate_idx:gate_idx + 1, :])
    o_ref[...] = x_ref[...] + gate * acc


def mm_resid(a, w, x, mod, gate_idx, lay, name="mm_resid"):
    m, k = a.shape
    n = w.shape[1]
    tn = 512
    tm = m // RESID_ROW_TILES
    assert m % RESID_ROW_TILES == 0 and tm % BF16_ROWS == 0
    return pl.pallas_call(
        functools.partial(_mm_resid_kernel, gate_idx=gate_idx, n_lat=lay.n_lat, tm=tm),
        grid=(RESID_ROW_TILES, n // tn),
        in_specs=[pl.BlockSpec((tm, k), lambda i, j: (i, 0)),
                  pl.BlockSpec((k, tn), lambda i, j: (0, j)),
                  pl.BlockSpec((tm, tn), lambda i, j: (i, j)),
                  pl.BlockSpec((2, 8, tn), lambda i, j: (0, 0, j))],
        out_specs=pl.BlockSpec((tm, tn), lambda i, j: (i, j)),
        out_shape=jax.ShapeDtypeStruct((m, n), F32),
        compiler_params=_cp(("parallel", "parallel")),
        name=name,
    )(a, w, x, mod)


HALO = BF16_ROWS


def _mm_conv_kernel(*refs, ksize, gated, act, tm, n_lat_tiles, n_tiles, gw):
    nw = 2 if gated else 1
    a_ref, ap_ref, an_ref = refs[:3]
    w_refs = refs[3:3 + nw]
    cw_refs = refs[3 + nw:3 + 2 * nw]
    cb_refs = refs[3 + 2 * nw:3 + 3 * nw]
    o_ref = refs[3 + 3 * nw]
    hbuf = refs[4 + 3 * nw]
    u_scrs = refs[5 + 3 * nw:]
    i = pl.program_id(0)
    tn = w_refs[0].shape[1]

    @pl.when(pl.program_id(1) == 0)
    def _():
        first = (i == 0) | (i == n_lat_tiles)
        last = (i == n_lat_tiles - 1) | (i == n_tiles - 1)
        prev = ap_ref[...]
        nxt = an_ref[...]
        hbuf[0:HALO, :] = jnp.where(first, jnp.zeros_like(prev), prev)
        hbuf[HALO:HALO + tm, :] = a_ref[...]
        hbuf[HALO + tm:, :] = jnp.where(last, jnp.zeros_like(nxt), nxt)

    left = (ksize - 1) // 2

    def conv(wi):
        u_scrs[wi][...] = _dot(hbuf[...], w_refs[wi][...])
        acc = cb_refs[wi][...] + cw_refs[wi][0:1, :] * u_scrs[wi][pl.ds(HALO - left, tm), :]
        for kk in range(1, ksize):
            acc = acc + cw_refs[wi][kk:kk + 1, :] * u_scrs[wi][pl.ds(HALO - left + kk, tm), :]
        return acc

    if gated:
        out = _silu(conv(0)) * conv(1)
    else:
        out = conv(0)
        if act == "silu":
            out = _silu(out)
    out = out.astype(o_ref.dtype)
    if gw is None:
        o_ref[...] = out
    else:
        for q in range(tn // gw):
            o_ref[q] = out[:, q * gw:(q + 1) * gw]


def mm_conv(a, w, conv_w, conv_b, lay, *, tn, out_dtype, gated=False, act=None,
            gw=None, col0=0, ncols=None, name="mm_conv"):
    tp, k = a.shape
    ksize = conv_w.shape[0]
    ncols = ncols if ncols is not None else w.shape[1]
    nj = ncols // tn
    c0 = col0 // tn
    assert col0 % tn == 0 and ncols % tn == 0
    assert gw is None or tn % gw == 0
    n_tiles = tp // TM
    hb = TM // HALO
    a_specs = [pl.BlockSpec((TM, k), lambda i, j: (i, 0)),
               pl.BlockSpec((HALO, k), lambda i, j: (jnp.maximum(i * hb - 1, 0), 0)),
               pl.BlockSpec((HALO, k), lambda i, j: (jnp.minimum((i + 1) * hb, tp // HALO - 1), 0))]
    offs = [c0, c0 + nj] if gated else [c0]
    w_specs = [pl.BlockSpec((k, tn), lambda i, j, o=o: (0, o + j)) for o in offs]
    cw_specs = [pl.BlockSpec((ksize, tn), lambda i, j, o=o: (0, o + j)) for o in offs]
    cb_specs = [pl.BlockSpec((1, tn), lambda i, j, o=o: (0, o + j)) for o in offs]
    nw = len(offs)
    if gw is not None:
        out_spec = pl.BlockSpec((tn // gw, TM, gw), lambda i, j: (j, i, 0))
        out_shape = jax.ShapeDtypeStruct((ncols // gw, tp, gw), out_dtype)
    else:
        out_spec = pl.BlockSpec((TM, tn), lambda i, j: (i, j))
        out_shape = jax.ShapeDtypeStruct((tp, ncols), out_dtype)
    cb2 = conv_b.reshape(1, -1)
    return pl.pallas_call(
        functools.partial(_mm_conv_kernel, ksize=ksize, gated=gated, act=act, tm=TM,
                          n_lat_tiles=lay.n_lat // TM, n_tiles=n_tiles, gw=gw),
        grid=(n_tiles, nj),
        in_specs=a_specs + w_specs + cw_specs + cb_specs,
        out_specs=out_spec,
        out_shape=out_shape,
        scratch_shapes=[pltpu.VMEM((TM + 2 * HALO, k), BF16)]
        + [pltpu.VMEM((TM + 2 * HALO, tn), F32) for _ in range(nw)],
        compiler_params=_cp(("parallel", "arbitrary")),
        name=name,
    )(a, a, a, *([w] * nw), *([conv_w] * nw), *([cb2] * nw))


def _chunk_index(k, lay, reverse, chunk):
    n_lc = lay.n_lat // chunk
    n_cc = lay.n_ctx // chunk
    if reverse:
        valid = n_lc + n_cc - 1 - k
    else:
        valid = jnp.where(k < n_cc, n_lc + k, k - n_cc)
    return jnp.where(k < n_lc + n_cc, valid, k)


N_SSD_IN = 11
SSD_GPS = 4


def _ssd_scan_kernel(*refs, n_valid_chunks):
    fwd_in, rev_in = refs[:N_SSD_IN], refs[N_SSD_IN:2 * N_SSD_IN]
    yf_ref, yb_ref, hf_scr, hb_scr = refs[2 * N_SSD_IN:]
    k = pl.program_id(1)

    @pl.when(k == 0)
    def _():
        hf_scr[...] = jnp.zeros_like(hf_scr)
        hb_scr[...] = jnp.zeros_like(hb_scr)

    @pl.when(k >= n_valid_chunks)
    def _():
        yf_ref[...] = jnp.zeros_like(yf_ref)
        yb_ref[...] = jnp.zeros_like(yb_ref)

    @pl.when(k < n_valid_chunks)
    def _():
        chains = []
        for gi in range(SSD_GPS):
            for ins, y_ref, h_scr, rev in ((fwd_in, yf_ref, hf_scr, False), (rev_in, yb_ref, hb_scr, True)):
                views = [r.at[gi] for r in ins]
                chains.append((_ssd_chunk_setup(*views, reverse=rev), views[0], views[-1],
                               y_ref.at[gi], h_scr.at[gi]))
        for p in range(SSD_HEADS_PER_GROUP // 2):
            for pre, xs_v, dsk_v, y_v, h_v in chains:
                _ssd_head_pair(p, pre, xs_v, dsk_v, y_v, h_v)


def _ssd_chunk_setup(xs_ref, b_ref, c_ref, bt_ref, dtc_ref, dtr_ref, biasc_ref, biasr_ref,
                     ac_ref, ar_ref, dsk_ref, *, reverse):
    L = SSD_CHUNK
    li = lax.broadcasted_iota(jnp.int32, (L, L), 0)
    si = lax.broadcasted_iota(jnp.int32, (L, L), 1)
    if reverse:
        mask = si >= li
        mask_t = li >= si
    else:
        mask = si <= li
        mask_t = li <= si
    tri = jnp.where(mask, 1.0, 0.0).astype(BF16)
    tri_t = jnp.where(mask_t, 1.0, 0.0).astype(BF16)

    dtc = _softplus(dtc_ref[...] + biasc_ref[...])
    adt_c = dtc * ac_ref[...]
    hi, mid, lo = _split3(adt_c)
    acs_c = _dot(tri, hi) + _dot(tri, mid) + _dot(tri, lo)
    dtr = _softplus(dtr_ref[...] + biasr_ref[...])
    adt_r = dtr * ar_ref[...]
    hi, mid, lo = _split3(adt_r)
    acs_r = _dot(hi, tri_t) + _dot(mid, tri_t) + _dot(lo, tri_t)
    tot_r = acs_r[:, 0:1] if reverse else acs_r[:, L - 1:L]
    wdt_r = jnp.exp(tot_r - acs_r) * dtr
    etot = jnp.exp(tot_r)
    ecol = jnp.exp(acs_c)

    cb16 = c_ref[...]
    bt16 = bt_ref[...]
    cbm = jnp.where(mask, _dot(cb16, bt16), 0.0)
    return dict(acs_c=acs_c, acs_r=acs_r, dtr=dtr, wdt_r=wdt_r, etot=etot, ecol=ecol, cbm=cbm,
                cf=cb16.astype(F32), btf=bt16.astype(F32))


def _ssd_head_pair(p, pre, xs_ref, dsk_ref, y_ref, h_scr):
    L = SSD_CHUNK
    lo_half = lax.broadcasted_iota(jnp.int32, (L, LANES), 1) < SSD_HEAD_DIM
    cols = slice(p * LANES, (p + 1) * LANES)
    xp = xs_ref[:, cols]
    xpb = xp.astype(BF16)
    hp = h_scr[p]
    hpb = hp.astype(BF16)
    ys, hs = [], []
    for e2 in range(2):
        e = 2 * p + e2
        col = pre["acs_c"][:, e:e + 1]
        row = pre["acs_r"][e:e + 1, :]
        ld = (pre["cbm"] * jnp.exp(jnp.minimum(col - row, 0.0)) * pre["dtr"][e:e + 1, :]).astype(BF16)
        lo_m = (pre["cf"] * pre["ecol"][:, e:e + 1]).astype(BF16)
        ls = (pre["btf"] * pre["wdt_r"][e:e + 1, :]).astype(BF16)
        ys.append(_dot(ld, xpb) + _dot(lo_m, hpb))
        hs.append(_dot(ls, xpb) + pre["etot"][e:e + 1, :] * hp)
    y_ref[:, cols] = jnp.where(lo_half, ys[0], ys[1]) + dsk_ref[:, cols] * xp
    h_scr[p] = jnp.where(lo_half, hs[0], hs[1])


def ssd_scan(xs, bc, bt, dt_cols, dt_rows, bias, a_neg, dskip, lay):
    g, tp, gw = xs.shape
    L = SSD_CHUNK
    n_chunks = tp // L
    n_valid = lay.n_valid // L
    hpg = SSD_HEADS_PER_GROUP

    def lane_pad(v):
        return jnp.pad(v.reshape(g, 1, hpg), ((0, 0), (0, 0), (0, LANES - hpg)))

    in_specs, args = [], []
    gps = SSD_GPS
    assert g % gps == 0
    for d in range(2):
        cidx = functools.partial(_chunk_index, lay=lay, reverse=(d == 1), chunk=L)
        in_specs += [pl.BlockSpec((gps, L, gw), lambda gi, k, c=cidx: (gi, c(k), 0)),
                     pl.BlockSpec((gps, L, SSD_STATE), lambda gi, k, c=cidx: (gi, c(k), 0)),
                     pl.BlockSpec((gps, L, SSD_STATE), lambda gi, k, c=cidx: (g // gps + gi, c(k), 0)),
                     pl.BlockSpec((gps, SSD_STATE, L), lambda gi, k, c=cidx: (gi, 0, c(k))),
                     pl.BlockSpec((gps, L, LANES), lambda gi, k, c=cidx: (gi, c(k), 0)),
                     pl.BlockSpec((gps, hpg, L), lambda gi, k, c=cidx: (gi, 0, c(k))),
                     pl.BlockSpec((gps, 1, LANES), lambda gi, k: (gi, 0, 0)),
                     pl.BlockSpec((gps, hpg, 1), lambda gi, k: (gi, 0, 0)),
                     pl.BlockSpec((gps, 1, LANES), lambda gi, k: (gi, 0, 0)),
                     pl.BlockSpec((gps, hpg, 1), lambda gi, k: (gi, 0, 0)),
                     pl.BlockSpec((gps, 1, gw), lambda gi, k: (gi, 0, 0))]
        args += [xs, bc, bc, bt, dt_cols[d], dt_rows[d], lane_pad(bias[d]), bias[d].reshape(g, hpg, 1),
                 lane_pad(a_neg[d]), a_neg[d].reshape(g, hpg, 1),
                 jnp.repeat(dskip[d], SSD_HEAD_DIM).reshape(g, 1, gw)]
    assert len(args) == 2 * N_SSD_IN
    out_specs = [pl.BlockSpec((gps, L, gw),
                              lambda gi, k, c=functools.partial(_chunk_index, lay=lay, reverse=(d == 1), chunk=L):
                              (gi, c(k), 0)) for d in range(2)]
    return pl.pallas_call(
        functools.partial(_ssd_scan_kernel, n_valid_chunks=n_valid),
        grid=(g // gps, n_chunks),
        in_specs=in_specs,
        out_specs=out_specs,
        out_shape=[jax.ShapeDtypeStruct((g, tp, gw), F32)] * 2,
        scratch_shapes=[pltpu.VMEM((gps, hpg // 2, SSD_STATE, LANES), F32)] * 2,
        compiler_params=_cp(("parallel", "arbitrary")),
        name="ssd_scan",
    )(*args)


def _ssd_gate_kernel(yf_ref, yb_ref, z_ref, nw_ref, a_ref):
    gw = yf_ref.shape[-1]
    for g in range(SSD_GROUPS):
        cols = slice(g * gw, (g + 1) * gw)
        gz = (yf_ref[g] + yb_ref[g]) * _silu(z_ref[:, cols])
        gn = gz * lax.rsqrt(jnp.mean(gz * gz, axis=-1, keepdims=True) + NORM_EPS)
        a_ref[:, cols] = (gn * nw_ref[:, cols]).astype(BF16)


def ssd_gate(yf, yb, z, norm_w):
    g, tp, gw = yf.shape
    d_inner = g * gw
    tm = TE
    return pl.pallas_call(
        _ssd_gate_kernel,
        grid=(tp // tm,),
        in_specs=[pl.BlockSpec((g, tm, gw), lambda i: (0, i, 0)),
                  pl.BlockSpec((g, tm, gw), lambda i: (0, i, 0)),
                  pl.BlockSpec((tm, d_inner), lambda i: (i, 0)),
                  pl.BlockSpec((1, d_inner), lambda i: (0, 0))],
        out_specs=pl.BlockSpec((tm, d_inner), lambda i: (i, 0)),
        out_shape=jax.ShapeDtypeStruct((tp, d_inner), BF16),
        compiler_params=_cp(("parallel",)),
        name="ssd_gate",
    )(yf, yb, z, norm_w.reshape(1, d_inner))


def ssd_mixer(h, lay, w_in, conv_w, conv_b, dt_bias, a_log, d_skip, norm_w):
    d_inner = SSD_GROUPS * SSD_HEADS_PER_GROUP * SSD_HEAD_DIM
    gn = SSD_GROUPS * SSD_STATE
    conv_dim = d_inner + 2 * gn
    n_heads = SSD_GROUPS * SSD_HEADS_PER_GROUP
    tp = h.shape[0]
    w16 = w_in.astype(BF16)
    w_rest = w16[:, d_inner:d_inner + conv_dim]
    z = mm(h, w16[:, :d_inner], tn=512, name="ssd_z")
    xs = mm_conv(h, w_rest, conv_w, conv_b, lay, tn=SSD_HEADS_PER_GROUP * SSD_HEAD_DIM, out_dtype=F32,
                 act="silu", gw=SSD_HEADS_PER_GROUP * SSD_HEAD_DIM, col0=0, ncols=d_inner, name="ssd_x")
    bc = mm_conv(h, w_rest, conv_w, conv_b, lay, tn=512, out_dtype=BF16, act="silu",
                 gw=SSD_STATE, col0=d_inner, ncols=2 * gn, name="ssd_bc")
    dt_pre = mm(h, w16[:, d_inner + conv_dim:], name="ssd_dt")
    bt = jnp.transpose(bc[:SSD_GROUPS], (0, 2, 1))
    a_neg = -jnp.exp(a_log.astype(F32))
    dt_cols, dt_rows = [], []
    for d in range(2):
        dt_d = dt_pre[:, d * n_heads:(d + 1) * n_heads].reshape(tp, SSD_GROUPS, SSD_HEADS_PER_GROUP)
        dt_cols.append(jnp.pad(jnp.transpose(dt_d, (1, 0, 2)),
                               ((0, 0), (0, 0), (0, LANES - SSD_HEADS_PER_GROUP))))
        dt_rows.append(jnp.transpose(dt_d, (1, 2, 0)))
    y_fwd, y_rev = ssd_scan(xs, bc, bt, dt_cols, dt_rows, dt_bias, a_neg, d_skip, lay)
    return ssd_gate(y_fwd, y_rev, z, norm_w)


LRU_TB = 128


def _lru_scan_kernel(u_ref, gw_ref, gb_ref, lam_ref, o_ref, carry, *, reverse, n_valid_chunks):
    k = pl.program_id(0)
    tb = LRU_TB
    bw = u_ref.shape[1] // LRU_BLOCKS

    @pl.when(k == 0)
    def _():
        carry[...] = jnp.zeros_like(carry)

    @pl.when(k >= n_valid_chunks)
    def _():
        o_ref[...] = jnp.zeros_like(o_ref)

    @pl.when(k < n_valid_chunks)
    def _():
        u = u_ref[...]
        ub = u.astype(BF16)

        def pre(z):
            parts = [_dot(ub[:, n * bw:(n + 1) * bw], gw_ref[z, n]) for n in range(LRU_BLOCKS)]
            return jnp.concatenate(parts, axis=1) + gb_ref[z]

        r = jax.nn.sigmoid(pre(0))
        gi = jax.nn.sigmoid(pre(1))
        log_a = (-LRU_C) * r * _softplus(-lam_ref[...])
        a = jnp.exp(log_a)
        th = jnp.tanh(log_a)
        b = jnp.sqrt(-2.0 * th / (1.0 - th)) * (gi * u)
        n_g = tb // SUBLANES
        w = a.shape[1]
        sub = lax.broadcasted_iota(jnp.int32, (tb, 1), 0) & (SUBLANES - 1)

        def roll_in_group(x, shift):
            return pltpu.roll(x.reshape(n_g, SUBLANES, w), shift, 1).reshape(tb, w)

        sh = 1
        while sh < SUBLANES:
            if reverse:
                keep = sub < SUBLANES - sh
                a_s = jnp.where(keep, roll_in_group(a, SUBLANES - sh), 1.0)
                b_s = jnp.where(keep, roll_in_group(b, SUBLANES - sh), 0.0)
            else:
                keep = sub >= sh
                a_s = jnp.where(keep, roll_in_group(a, sh), 1.0)
                b_s = jnp.where(keep, roll_in_group(b, sh), 0.0)
            b = a * b_s + b
            a = a * a_s
            sh *= 2
        c = carry[0:1, :]
        for gi in (range(n_g - 1, -1, -1) if reverse else range(n_g)):
            rows = slice(gi * SUBLANES, (gi + 1) * SUBLANES)
            hg = b[rows, :] + a[rows, :] * c
            o_ref[rows, :] = hg
            c = hg[0:1, :] if reverse else hg[SUBLANES - 1:SUBLANES, :]
        carry[0:1, :] = c


def lru_scan(u, gate_w, gate_b, lam, lay, reverse):
    tp, w = u.shape
    tb = LRU_TB
    bw = w // LRU_BLOCKS
    cidx = functools.partial(_chunk_index, lay=lay, reverse=reverse, chunk=tb)
    return pl.pallas_call(
        functools.partial(_lru_scan_kernel, reverse=reverse, n_valid_chunks=lay.n_valid // tb),
        grid=(tp // tb,),
        in_specs=[pl.BlockSpec((tb, w), lambda k: (cidx(k), 0)),
                  pl.BlockSpec((2, LRU_BLOCKS, bw, bw), lambda k: (0, 0, 0, 0)),
                  pl.BlockSpec((2, 1, w), lambda k: (0, 0, 0)),
                  pl.BlockSpec((1, w), lambda k: (0, 0))],
        out_specs=pl.BlockSpec((tb, w), lambda k: (cidx(k), 0)),
        out_shape=jax.ShapeDtypeStruct((tp, w), F32),
        scratch_shapes=[pltpu.VMEM((8, w), F32)],
        compiler_params=_cp(("arbitrary",)),
        name="lru_scan_rev" if reverse else "lru_scan_fwd",
    )(u, gate_w.astype(BF16), gate_b.reshape(2, 1, w), lam.reshape(1, w))


def _lru_gate_kernel(hf_ref, hb_ref, gp_ref, a_ref):
    a_ref[...] = ((hf_ref[...] + hb_ref[...]) * jax.nn.gelu(gp_ref[...], approximate=True)).astype(BF16)


def lru_gate(hf, hb, gate_pre):
    tp, w = hf.shape
    tm = TE
    spec = pl.BlockSpec((tm, w), lambda i: (i, 0))
    return pl.pallas_call(
        _lru_gate_kernel,
        grid=(tp // tm,),
        in_specs=[spec, spec, spec],
        out_specs=spec,
        out_shape=jax.ShapeDtypeStruct((tp, w), BF16),
        compiler_params=_cp(("parallel",)),
        name="lru_gate",
    )(hf, hb, gate_pre)


def lru_mixer(h, lay, w_in, conv_w, conv_b, gate_w, gate_b, lam):
    w = conv_w.shape[1]
    w16 = w_in.astype(BF16)
    gate_pre = mm(h, w16[:, :w], tn=512, name="lru_gate_proj")
    u = mm_conv(h, w16[:, w:], conv_w, conv_b, lay, tn=512, out_dtype=F32, name="lru_x")
    hs = [lru_scan(u, gate_w[d], gate_b[d], lam[d], lay, reverse=(d == 1)) for d in range(2)]
    return lru_gate(hs[0], hs[1], gate_pre)


def rope_tables(lay):
    n = lay.n_lat
    row = jnp.repeat(jnp.arange(n // GRID_W, dtype=F32), GRID_W)
    col = jnp.tile(jnp.arange(GRID_W, dtype=F32), n // GRID_W)
    n_freq = ROPE_DIM // 4
    inv = ROPE_THETA ** (-jnp.arange(n_freq, dtype=F32) / n_freq)
    ang = jnp.concatenate([row[:, None] * inv, col[:, None] * inv], axis=-1)
    cos, sin = jnp.cos(ang), jnp.sin(ang)
    zero = jnp.zeros_like(sin)
    c = jnp.tile(cos, (1, 4))
    s1 = jnp.tile(jnp.concatenate([-sin, zero], axis=1), (1, 2))
    s2 = jnp.tile(jnp.concatenate([zero, sin], axis=1), (1, 2))
    pad = lay.tp - n
    c = jnp.concatenate([c, jnp.ones((pad, LANES), F32)], axis=0)
    s1 = jnp.pad(s1, ((0, pad), (0, 0)))
    s2 = jnp.pad(s2, ((0, pad), (0, 0)))
    return c, s1, s2


MLA_QK_PAD = 256
MLA_HPS = 2


def _mla_q_kernel(a_ref, g_ref, w_ref, c_ref, s1_ref, s2_ref, qr_ref, qp_ref, nscr, *, scale):
    @pl.when(pl.program_id(1) == 0)
    def _():
        x = a_ref[...]
        nscr[...] = (x * lax.rsqrt(jnp.mean(x * x, axis=-1, keepdims=True) + NORM_EPS) * g_ref[...]).astype(BF16)

    acc = _dot(nscr[...], w_ref[...]) * scale
    c, s1, s2 = c_ref[...], s1_ref[...], s2_ref[...]
    for hh in range(MLA_HPS):
        a_h = acc[:, hh * MLA_QK_PAD:(hh + 1) * MLA_QK_PAD]
        qp_ref[hh] = a_h.astype(BF16)
        rot = _rope128(a_h[:, LANES:], c, s1, s2)
        qr_ref[hh] = jnp.concatenate([a_h[:, :LANES], rot], axis=1).astype(BF16)


def _mla_kv_kernel(a_ref, kr_ref, g_ref, w_ref, c_ref, s1_ref, s2_ref, k_ref, v_ref, nscr, krscr):
    @pl.when(pl.program_id(1) == 0)
    def _():
        x = a_ref[...]
        nscr[...] = (x * lax.rsqrt(jnp.mean(x * x, axis=-1, keepdims=True) + NORM_EPS) * g_ref[...]).astype(BF16)
        krscr[...] = _rope128(kr_ref[...], c_ref[...], s1_ref[...], s2_ref[...]).astype(BF16)

    acc = _dot(nscr[...], w_ref[...])
    ones = jnp.ones((acc.shape[0], LANES), BF16)
    for hh in range(MLA_HPS):
        a_h = acc[:, hh * (MLA_NOPE + MLA_V):(hh + 1) * (MLA_NOPE + MLA_V)]
        k_ref[hh] = jnp.concatenate([a_h[:, :MLA_NOPE].astype(BF16), krscr[...]], axis=1)
        v_ref[hh] = jnp.concatenate([a_h[:, MLA_NOPE:].astype(BF16), ones], axis=1)


MLA_TQ = 1024
MLA_TK = 512
MLA_SUB = 256


def _mla_ctx_attn_kernel(qp_ref, k_ref, v_ref, o_ref):
    s = _dot_nt(qp_ref[...], k_ref[...])
    p = jnp.exp2(s - jnp.max(s, axis=-1, keepdims=True)).astype(BF16)
    acc = _dot(p, v_ref[...])
    o_ref[...] = (acc[:, :MLA_V] / acc[:, MLA_V:]).astype(o_ref.dtype)


def _mla_attn_kernel(qr_ref, qp_ref, k_ref, v_ref, o_ref, acc_ref, s_ref, *, n_lat, n_ctx):
    tq, tk, sub = MLA_TQ, MLA_TK, MLA_SUB
    nsub = tq // sub
    n_kt = n_lat // tk
    assert n_kt % 2 == 0
    rows = [slice(c * sub, (c + 1) * sub) for c in range(nsub)]

    def qk_lat(c, tile):
        start = pl.multiple_of(tile * tk, tk)
        return _dot_nt(qr_ref[rows[c], :], k_ref[pl.ds(start, tk), :])

    def consume(c, s, vv, m):
        m_new = jnp.maximum(m, jnp.max(s, axis=-1, keepdims=True))
        alpha = jnp.exp2(m - m_new)
        p = jnp.exp2((s - m_new).astype(BF16))
        acc_ref[rows[c], :] = alpha * acc_ref[rows[c], :] + _dot(p, vv)
        return m_new

    def v_tile(tile):
        return v_ref[pl.ds(pl.multiple_of(tile * tk, tk), tk), :]

    def half(ms, slot, tile):
        vv = v_tile(tile)
        out = []
        for c in range(nsub):
            s = s_ref[slot, rows[c], :]
            s_ref[1 - slot, rows[c], :] = qk_lat(c, tile + 1)
            out.append(consume(c, s, vv, ms[c]))
        return tuple(out)

    def finish():
        acc = acc_ref[...]
        o_ref[...] = (acc[:, :MLA_V] / acc[:, MLA_V:]).astype(o_ref.dtype)

    acc_ref[...] = jnp.zeros_like(acc_ref)
    m0 = tuple(jnp.full((sub, 1), -jnp.inf, F32) for _ in range(nsub))
    k_ctx = lambda: k_ref[pl.ds(n_lat, n_ctx), :]
    v_ctx = lambda: v_ref[pl.ds(n_lat, n_ctx), :]

    for c in range(nsub):
        s_ref[0, rows[c], :] = qk_lat(c, 0)

    def pair(pi, ms):
        ms = half(ms, 0, 2 * pi)
        return half(ms, 1, 2 * pi + 1)

    ms = lax.fori_loop(0, n_kt // 2 - 1, pair, m0)
    ms = half(ms, 0, n_kt - 2)
    vv = v_tile(n_kt - 1)
    for c in range(nsub):
        s = s_ref[1, rows[c], :]
        s_c = _dot_nt(qp_ref[rows[c], :], k_ctx())
        m = consume(c, s, vv, ms[c])
        consume(c, s_c, v_ctx(), m)
    finish()


def mla_mixer(h, lay, tables, w_in, q_norm, kv_norm, w_q_up, w_kv_up):
    tp = h.shape[0]
    d = h.shape[1]
    c, s1, s2 = tables
    qk = MLA_NOPE + ROPE_DIM
    w_lat = jnp.concatenate([
        w_in[:, :MLA_Q_RANK], jnp.zeros((d, 256), F32),
        w_in[:, MLA_Q_RANK:MLA_Q_RANK + MLA_KV_RANK],
        w_in[:, MLA_Q_RANK + MLA_KV_RANK:], jnp.zeros((d, LANES - ROPE_DIM), F32)], axis=1).astype(BF16)
    lat = mm(h, w_lat, name="mla_lat")
    wq = jnp.pad(w_q_up.reshape(MLA_Q_RANK, MLA_HEADS, qk),
                 ((0, 0), (0, 0), (0, MLA_QK_PAD - qk))).reshape(MLA_Q_RANK, MLA_HEADS * MLA_QK_PAD).astype(BF16)
    n_t = tp // TM
    tab_spec = pl.BlockSpec((TM, LANES), lambda i, j: (i, 0))
    q_rot, q_plain = pl.pallas_call(
        functools.partial(_mla_q_kernel, scale=qk ** -0.5 * LOG2E),
        grid=(n_t, MLA_HEADS // MLA_HPS),
        in_specs=[pl.BlockSpec((TM, MLA_Q_RANK), lambda i, j: (i, 0)),
                  pl.BlockSpec((1, MLA_Q_RANK), lambda i, j: (0, 0)),
                  pl.BlockSpec((MLA_Q_RANK, MLA_HPS * MLA_QK_PAD), lambda i, j: (0, j)),
                  tab_spec, tab_spec, tab_spec],
        out_specs=[pl.BlockSpec((MLA_HPS, TM, MLA_QK_PAD), lambda i, j: (j, i, 0)),
                   pl.BlockSpec((MLA_HPS, TM, MLA_QK_PAD), lambda i, j: (j, i, 0))],
        out_shape=[jax.ShapeDtypeStruct((MLA_HEADS, tp, MLA_QK_PAD), BF16)] * 2,
        scratch_shapes=[pltpu.VMEM((TM, MLA_Q_RANK), BF16)],
        compiler_params=_cp(("parallel", "arbitrary")),
        name="mla_q",
    )(lat, q_norm.reshape(1, -1), wq, c, s1, s2)
    kv_col = (MLA_Q_RANK + 256) // MLA_KV_RANK
    kr_col = (MLA_Q_RANK + 256 + MLA_KV_RANK) // LANES
    k_all, v_all = pl.pallas_call(
        _mla_kv_kernel,
        grid=(n_t, MLA_HEADS // MLA_HPS),
        in_specs=[pl.BlockSpec((TM, MLA_KV_RANK), lambda i, j: (i, kv_col)),
                  pl.BlockSpec((TM, LANES), lambda i, j: (i, kr_col)),
                  pl.BlockSpec((1, MLA_KV_RANK), lambda i, j: (0, 0)),
                  pl.BlockSpec((MLA_KV_RANK, MLA_HPS * (MLA_NOPE + MLA_V)), lambda i, j: (0, j)),
                  tab_spec, tab_spec, tab_spec],
        out_specs=[pl.BlockSpec((MLA_HPS, TM, MLA_QK_PAD), lambda i, j: (j, i, 0)),
                   pl.BlockSpec((MLA_HPS, TM, MLA_V + LANES), lambda i, j: (j, i, 0))],
        out_shape=[jax.ShapeDtypeStruct((MLA_HEADS, tp, MLA_QK_PAD), BF16),
                   jax.ShapeDtypeStruct((MLA_HEADS, tp, MLA_V + LANES), BF16)],
        scratch_shapes=[pltpu.VMEM((TM, MLA_KV_RANK), BF16), pltpu.VMEM((TM, LANES), BF16)],
        compiler_params=_cp(("parallel", "arbitrary")),
        name="mla_kv",
    )(lat, lat, kv_norm.reshape(1, -1), w_kv_up.astype(BF16), c, s1, s2)
    tq = min(MLA_TQ, lay.n_lat)
    assert lay.n_lat % tq == 0 and tq % MLA_SUB == 0
    o_lat = pl.pallas_call(
        functools.partial(_mla_attn_kernel, n_lat=lay.n_lat, n_ctx=lay.n_ctx),
        grid=(MLA_HEADS, lay.n_lat // tq),
        in_specs=[pl.BlockSpec((None, tq, MLA_QK_PAD), lambda hh, i: (hh, i, 0)),
                  pl.BlockSpec((None, tq, MLA_QK_PAD), lambda hh, i: (hh, i, 0)),
                  pl.BlockSpec((None, tp, MLA_QK_PAD), lambda hh, i: (hh, 0, 0)),
                  pl.BlockSpec((None, tp, MLA_V + LANES), lambda hh, i: (hh, 0, 0))],
        out_specs=pl.BlockSpec((tq, MLA_V), lambda hh, i: (i, hh)),
        out_shape=jax.ShapeDtypeStruct((lay.n_lat, MLA_HEADS * MLA_V), BF16),
        scratch_shapes=[pltpu.VMEM((tq, MLA_V + LANES), F32), pltpu.VMEM((2, tq, MLA_TK), F32)],
        compiler_params=_cp(("parallel", "parallel")),
        name="mla_attn",
    )(q_rot, q_plain, k_all, v_all)
    ctx_rows = tp - lay.n_lat
    ctx_blk = lay.n_lat // lay.n_ctx
    o_ctx = pl.pallas_call(
        _mla_ctx_attn_kernel,
        grid=(MLA_HEADS,),
        in_specs=[pl.BlockSpec((None, ctx_rows, MLA_QK_PAD), lambda hh: (hh, lay.n_lat // ctx_rows, 0)),
                  pl.BlockSpec((None, lay.n_ctx, MLA_QK_PAD), lambda hh: (hh, ctx_blk, 0)),
                  pl.BlockSpec((None, lay.n_ctx, MLA_V + LANES), lambda hh: (hh, ctx_blk, 0))],
        out_specs=pl.BlockSpec((ctx_rows, MLA_V), lambda hh: (0, hh)),
        out_shape=jax.ShapeDtypeStruct((ctx_rows, MLA_HEADS * MLA_V), BF16),
        compiler_params=_cp(("parallel",)),
        name="mla_attn_ctx",
    )(q_plain, k_all, v_all)
    return jnp.concatenate([o_lat, o_ctx], axis=0)


def _mm_bias_rope_kernel(*refs, scale, rope, emit_plain):
    a_ref, w_ref, b_ref = refs[:3]
    if rope:
        c_ref, s1_ref, s2_ref = refs[3:6]
        outs = refs[6:]
    else:
        outs = refs[3:]
    acc = (_dot(a_ref[...], w_ref[...]) + b_ref[...]) * scale
    if not rope:
        outs[0][...] = acc.astype(outs[0].dtype)
        return
    c, s1, s2 = c_ref[...], s1_ref[...], s2_ref[...]
    tn = acc.shape[1]
    for s in range(tn // LANES):
        cols = slice(s * LANES, (s + 1) * LANES)
        outs[0][:, cols] = _rope128(acc[:, cols], c, s1, s2).astype(outs[0].dtype)
    if emit_plain:
        outs[1][...] = acc.astype(outs[1].dtype)


def mm_bias_rope(a, w, b, tables, *, scale=1.0, rope=True, emit_plain=False, name="mm_bias_rope"):
    m, k = a.shape
    n = w.shape[1]
    tn = 512
    tm = m // RESID_ROW_TILES
    assert m % RESID_ROW_TILES == 0 and tm % BF16_ROWS == 0
    in_specs = [pl.BlockSpec((tm, k), lambda i, j: (i, 0)),
                pl.BlockSpec((k, tn), lambda i, j: (0, j)),
                pl.BlockSpec((1, tn), lambda i, j: (0, j))]
    args = [a, w, b.reshape(1, n)]
    if rope:
        in_specs += [pl.BlockSpec((tm, LANES), lambda i, j: (i, 0))] * 3
        args += list(tables)
    n_out = 2 if (rope and emit_plain) else 1
    out_spec = pl.BlockSpec((tm, tn), lambda i, j: (i, j))
    res = pl.pallas_call(
        functools.partial(_mm_bias_rope_kernel, scale=scale, rope=rope, emit_plain=emit_plain),
        grid=(RESID_ROW_TILES, n // tn),
        in_specs=in_specs,
        out_specs=[out_spec] * n_out,
        out_shape=[jax.ShapeDtypeStruct((m, n), BF16)] * n_out,
        compiler_params=_cp(("parallel", "parallel")),
        name=name,
    )(*args)
    return res if n_out == 2 else res[0]


SWA_BPS = 4


def _swa_attn_kernel(qr_ref, qp_ref, kp_ref, kc_ref, kn_ref, vp_ref, vc_ref, vn_ref,
                     kctx_ref, vctx_ref, sink_ref, o_ref, *, n_lat):
    b = pl.program_id(1)
    blk = ATTN_BLOCK
    nb = n_lat // blk

    @pl.when(b * SWA_BPS >= nb)
    def _():
        o_ref[...] = jnp.zeros_like(o_ref)

    @pl.when(b * SWA_BPS < nb)
    def _():
        lane = lax.broadcasted_iota(jnp.int32, (blk, LANES), 1)
        r = lax.broadcasted_iota(jnp.int32, (blk, blk), 0)
        c = lax.broadcasted_iota(jnp.int32, (blk, blk), 1)
        lo_half = lane < SWA_HEAD_DIM
        kwin = jnp.concatenate([kp_ref[...], kc_ref[...], kn_ref[...]], axis=0)
        vwin = jnp.concatenate([vp_ref[...], vc_ref[...], vn_ref[...]], axis=0)
        ones_w = jnp.ones(vwin.shape, BF16)
        vwe = jnp.concatenate([vwin, ones_w], axis=1)
        vce = jnp.concatenate([vctx_ref[...], ones_w[:vctx_ref.shape[0]]], axis=1)
        kctx = kctx_ref[...]
        n_c = kctx.shape[0]
        scores = []
        for t in range(SWA_BPS):
            gb = b * SWA_BPS + t
            rows = slice(t * blk, (t + 1) * blk)
            win = slice(t * blk, (t + 3) * blk)
            bias_p = jnp.where((c >= r) & (gb > 0), 0.0, -jnp.inf)
            bias_n = jnp.where((c <= r) & (gb < nb - 1), 0.0, -jnp.inf)
            for g in range(SWA_GROUP):
                cols = slice((g // 2) * LANES, (g // 2 + 1) * LANES)
                keep = lo_half if g % 2 == 0 else jnp.logical_not(lo_half)
                qr = qr_ref[rows, cols]
                qp = qp_ref[rows, cols]
                qr = jnp.where(keep, qr, jnp.zeros_like(qr))
                qp = jnp.where(keep, qp, jnp.zeros_like(qp))
                s_c = _dot_nt(qp, kctx)
                s_w = _dot_nt(qr, kwin[win])
                scores.append(jnp.concatenate([s_c, s_w[:, :blk] + bias_p, s_w[:, blk:2 * blk],
                                               s_w[:, 2 * blk:] + bias_n], axis=1))
        for t in range(SWA_BPS):
            rows = slice(t * blk, (t + 1) * blk)
            win = slice(t * blk, (t + 3) * blk)
            outs = []
            for g in range(SWA_GROUP):
                s = scores[t * SWA_GROUP + g]
                sk = sink_ref[g:g + 1, 0:1] * LOG2E
                m = jnp.maximum(jnp.max(s, axis=-1, keepdims=True), sk)
                p = jnp.exp2(s - m).astype(BF16)
                oe = _dot(p[:, :n_c], vce) + _dot(p[:, n_c:], vwe[win])
                outs.append(oe[:, :LANES] / (oe[:, LANES:] + jnp.exp2(sk - m)))
            o_ref[rows, :] = jnp.concatenate(
                [jnp.where(lo_half, outs[0], outs[1]), jnp.where(lo_half, outs[2], outs[3])],
                axis=1).astype(o_ref.dtype)


def swa_mixer(h, lay, tables, w_qkv, b_qkv, sink):
    tp = h.shape[0]
    d = h.shape[1]
    hd = SWA_HEAD_DIM
    qd = SWA_Q_HEADS * hd
    kd = SWA_KV_HEADS * hd

    def dup_heads(wm):
        lead = wm.shape[:-1]
        w3 = wm.reshape(lead + (SWA_KV_HEADS, 1, hd))
        return jnp.broadcast_to(w3, lead + (SWA_KV_HEADS, 2, hd)).reshape(lead + (2 * kd,))

    w16 = w_qkv.astype(BF16)
    q_rot, q_plain = mm_bias_rope(h, w16[:, :qd], b_qkv[:qd], tables, scale=hd ** -0.5 * LOG2E,
                                  emit_plain=True, name="swa_q")
    k_dup = mm_bias_rope(h, dup_heads(w16[:, qd:qd + kd]), dup_heads(b_qkv[qd:qd + kd]), tables, name="swa_k")
    v_dup = mm_bias_rope(h, dup_heads(w16[:, qd + kd:]), dup_heads(b_qkv[qd + kd:]), tables,
                         rope=False, name="swa_v")
    blk = ATTN_BLOCK
    nb = lay.n_lat // blk
    ctx_blk = lay.n_lat // lay.n_ctx
    sink_t = jnp.broadcast_to(jnp.pad(sink.astype(F32).reshape(SWA_KV_HEADS, SWA_GROUP),
                                      ((0, 0), (0, 8 - SWA_GROUP)))[:, :, None], (SWA_KV_HEADS, 8, LANES))
    qw = SWA_GROUP * hd
    bps = SWA_BPS
    assert nb % bps == 0 and tp % (blk * bps) == 0
    q_spec = pl.BlockSpec((blk * bps, qw), lambda hh, b: (b, hh))
    prev_spec = pl.BlockSpec((blk, LANES), lambda hh, b: (jnp.clip(b * bps - 1, 0, nb - 1), hh))
    cur_spec = pl.BlockSpec((blk * bps, LANES), lambda hh, b: (jnp.minimum(b, nb // bps - 1), hh))
    next_spec = pl.BlockSpec((blk, LANES), lambda hh, b: (jnp.clip(b * bps + bps, 0, nb - 1), hh))
    ctx_spec = pl.BlockSpec((lay.n_ctx, LANES), lambda hh, b: (ctx_blk, hh))
    return pl.pallas_call(
        functools.partial(_swa_attn_kernel, n_lat=lay.n_lat),
        grid=(SWA_KV_HEADS, tp // (blk * bps)),
        in_specs=[q_spec, q_spec, prev_spec, cur_spec, next_spec,
                  prev_spec, cur_spec, next_spec, ctx_spec, ctx_spec,
                  pl.BlockSpec((None, 8, LANES), lambda hh, b: (hh, 0, 0))],
        out_specs=pl.BlockSpec((blk * bps, qw), lambda hh, b: (b, hh)),
        out_shape=jax.ShapeDtypeStruct((tp, qd), BF16),
        compiler_params=_cp(("parallel", "parallel")),
        name="swa_attn",
    )(q_rot, q_plain, k_dup, k_dup, k_dup, v_dup, v_dup, v_dup, k_dup, v_dup, sink_t)


def conv_ffn_hidden(h, lay, w_up, conv_w, conv_b):
    hidden = w_up.shape[1] // 2
    return mm_conv(h, w_up.astype(BF16), conv_w, conv_b, lay, tn=512, out_dtype=BF16, gated=True,
                   ncols=hidden, name="ffn_up")


def kernel(x, c, ctx, c_ctx, ada_w, ada_b, norm_mix, norm_ffn, ffn_up, ffn_conv_w, ffn_conv_b, ffn_down, final_norm, ssd_in, ssd_conv_w, ssd_conv_b, ssd_dt_bias, ssd_a_log, ssd_d, ssd_norm, ssd_out, lru_in, lru_conv_w, lru_conv_b, lru_gate_w, lru_gate_b, lru_lambda, lru_out, mla_in, mla_q_norm, mla_kv_norm, mla_q_up, mla_kv_up, mla_out, swa_qkv, swa_qkv_b, swa_sink, swa_out):
    bsz, n_lat, d = x.shape
    n_ctx = ctx.shape[1]
    depth = ada_w.shape[0]
    assert bsz == 1 and depth == 4, "one sample, four layers (one of each mixer)"
    lay = make_layout(n_lat, n_ctx)
    xs = jnp.concatenate([x[0], ctx[0], jnp.zeros((lay.tp - lay.n_valid, d), x.dtype)], axis=0)
    cv = jnp.concatenate([c, c_ctx[None, :], jnp.zeros((6, d), F32)], axis=0)
    mods = adaln(cv, ada_w, ada_b)
    tables = rope_tables(lay)
    for i in range(depth):
        mod = jnp.pad(mods[i, :2].reshape(2, ADA_CHUNKS, d), ((0, 0), (0, 8 - ADA_CHUNKS), (0, 0)))
        h = norm_mod(xs, norm_mix[i], mod, 0, 1, lay)
        if i == 0:
            a = ssd_mixer(h, lay, ssd_in[0], ssd_conv_w[0], ssd_conv_b[0], ssd_dt_bias[0], ssd_a_log[0],
                          ssd_d[0], ssd_norm[0])
            w_o = ssd_out[0]
        elif i == 1:
            a = lru_mixer(h, lay, lru_in[0], lru_conv_w[0], lru_conv_b[0], lru_gate_w[0], lru_gate_b[0],
                          lru_lambda[0])
            w_o = lru_out[0]
        elif i == 2:
            a = mla_mixer(h, lay, tables, mla_in[0], mla_q_norm[0], mla_kv_norm[0], mla_q_up[0], mla_kv_up[0])
            w_o = mla_out[0]
        else:
            a = swa_mixer(h, lay, tables, swa_qkv[0], swa_qkv_b[0], swa_sink[0])
            w_o = swa_out[0]
        xs = mm_resid(a, w_o.astype(BF16), xs, mod, 2, lay, name="mix_out")
        h = norm_mod(xs, norm_ffn[i], mod, 3, 4, lay)
        hid = conv_ffn_hidden(h, lay, ffn_up[i], ffn_conv_w[i], ffn_conv_b[i])
        xs = mm_resid(hid, ffn_down[i].astype(BF16), xs, mod, 5, lay, name="ffn_down")
    return rmsnorm_rows(xs, final_norm, n_lat)[None]
```
